```python
import math
import jax, jax.numpy as jnp
from jax import lax
import numpy as np

D_MODEL = 4096
BATCH = 4
SEQ = 2048
DEPTH = 4
DEC_BATCH = 8
DEC_SEQ = 4
PAST_LEN = 8192
PAGE_SIZE = 128

N_A_LAYERS = DEPTH // 2
N_B_LAYERS = DEPTH - N_A_LAYERS
A_HEADS = 8
A_HEAD_DIM = D_MODEL // A_HEADS
A_CHUNK = 64
A_PROJ = 4 * D_MODEL + 2 * A_HEADS
B_HEADS = 16
B_QK_DIM = D_MODEL // (2 * B_HEADS)
B_V_DIM = 2 * B_QK_DIM
B_QBLOCK = 128
KV_WIDTH = B_HEADS * (2 * B_QK_DIM + B_V_DIM)
D_FF = ((8 * D_MODEL // 3 + 255) // 256) * 256
N_NORMS = 6
EPS = 1e-6

kernel_name = "yoco_mlstm_diffattn_macaron_step"


def rms_norm(x, g):
    xf = x.astype(jnp.float32)
    y = xf * lax.rsqrt(jnp.mean(xf * xf, axis=-1, keepdims=True) + EPS)
    return (y * g.astype(jnp.float32)).astype(x.dtype)


def swiglu(x, w_gate, w_up, w_down):
    return (jax.nn.silu(x @ w_gate) * (x @ w_up)) @ w_down


def lambda_init(layer):
    return 0.8 - 0.6 * math.exp(-0.3 * layer)


def alibi_slopes():
    return jnp.asarray(2.0 ** (-8.0 * np.arange(1, B_HEADS + 1) / B_HEADS), jnp.float32)


def mlstm_chunk_step(carry, xs):
    C, n, m = carry
    q, k, v, li, lf = xs
    L = q.shape[2]
    b = jnp.cumsum(lf, axis=-1)
    causal = jnp.tril(jnp.ones((L, L), dtype=bool))
    log_d = jnp.where(causal, b[..., :, None] - b[..., None, :] + li[..., None, :], -jnp.inf)
    m_inter = b + m[..., None]
    m_t = jnp.maximum(m_inter, jnp.max(log_d, axis=-1))
    s = jnp.einsum('bhtd,bhsd->bhts', q, k) * jnp.exp(log_d - m_t[..., None])
    inter = jnp.exp(m_inter - m_t)
    num = jnp.einsum('bhts,bhsv->bhtv', s, v) + inter[..., None] * jnp.einsum('bhvd,bhtd->bhtv', C, q)
    den = jnp.sum(s, axis=-1) + inter * jnp.einsum('bhd,bhtd->bht', n, q)
    h = num / jnp.maximum(jnp.abs(den), jnp.exp(-m_t))[..., None]
    b_last = b[..., -1]
    log_w = b_last[..., None] - b + li
    m_new = jnp.maximum(b_last + m, jnp.max(log_w, axis=-1))
    w = jnp.exp(log_w - m_new[..., None])
    decay = jnp.exp(b_last + m - m_new)
    C_new = decay[..., None, None] * C + jnp.einsum('bhs,bhsv,bhsd->bhvd', w, v, k)
    n_new = decay[..., None] * n + jnp.einsum('bhs,bhsd->bhd', w, k)
    return (C_new, n_new, m_new), h


def mlstm_mixer(xn, w_in, b_if, head_g, w_out, C0, n0, m0):
    bsz, T, _ = xn.shape
    D = D_MODEL
    f32 = jnp.float32
    proj = xn @ w_in
    q, k, v, o_pre = proj[..., :D], proj[..., D:2 * D], proj[..., 2 * D:3 * D], proj[..., 3 * D:4 * D]
    gates = proj[..., 4 * D:].astype(f32) + b_if.astype(f32)
    i_pre, f_pre = gates[..., :A_HEADS], gates[..., A_HEADS:]

    def heads(t):
        return t.reshape(bsz, T, A_HEADS, A_HEAD_DIM).astype(f32).transpose(0, 2, 1, 3)

    q, k, v = heads(q), heads(k) * (A_HEAD_DIM ** -0.5), heads(v)
    li = i_pre.transpose(0, 2, 1)
    lf = jax.nn.log_sigmoid(f_pre).transpose(0, 2, 1)
    L = A_CHUNK if T % A_CHUNK == 0 else T
    nc = T // L

    def chunks(t):
        return jnp.moveaxis(t.reshape(t.shape[:2] + (nc, L) + t.shape[3:]), 2, 0)

    carry0 = (C0.astype(f32), n0.astype(f32), m0.astype(f32))
    (C, n, m), h = lax.scan(mlstm_chunk_step, carry0,
                            (chunks(q), chunks(k), chunks(v), chunks(li), chunks(lf)))
    h = jnp.moveaxis(h, 0, 2).reshape(bsz, A_HEADS, T, A_HEAD_DIM).transpose(0, 2, 1, 3)
    h = rms_norm(h, head_g) * jax.nn.sigmoid(o_pre.astype(f32)).reshape(bsz, T, A_HEADS, A_HEAD_DIM)
    out = h.reshape(bsz, T, D).astype(xn.dtype) @ w_out
    return out, C, n, m


def diff_attn_core(q, k, v, q_pos, k_pos, lam, slopes):
    s = jnp.einsum('bqhjd,bkhjd->jbhqk', q, k).astype(jnp.float32) * (B_QK_DIM ** -0.5)
    dist = (q_pos[:, None] - k_pos[None, :]).astype(jnp.float32)
    s = jnp.where(dist >= 0, s - slopes[:, None, None] * dist, -jnp.inf)
    p = jax.nn.softmax(s, axis=-1)
    a = p[0] - lam * p[1]
    return jnp.einsum('bhqk,bkhv->bqhv', a.astype(v.dtype), v)


def diff_attn_mixer(xn, w_q, lam_p, head_g, w_o, k, v, layer, past_len):
    bsz, T, _ = xn.shape
    q = (xn @ w_q).reshape(bsz, T, B_HEADS, 2, B_QK_DIM)
    lp = lam_p.astype(jnp.float32)
    lam0 = lambda_init(layer)
    lam = jnp.exp(jnp.sum(lp[0] * lp[1])) - jnp.exp(jnp.sum(lp[2] * lp[3])) + lam0
    slopes = alibi_slopes()
    q_pos = past_len + jnp.arange(T, dtype=jnp.int32)
    k_pos = jnp.arange(k.shape[1], dtype=jnp.int32)
    if past_len == 0:
        nb = T // B_QBLOCK
        qb = jnp.moveaxis(q.reshape(bsz, nb, B_QBLOCK, B_HEADS, 2, B_QK_DIM), 1, 0)
        pb = q_pos.reshape(nb, B_QBLOCK)
        o = lax.map(lambda a: diff_attn_core(a[0], k, v, a[1], k_pos, lam, slopes), (qb, pb))
        o = jnp.moveaxis(o, 0, 1).reshape(bsz, T, B_HEADS, B_V_DIM)
    else:
        o = diff_attn_core(q, k, v, q_pos, k_pos, lam, slopes)
    o = rms_norm(o, head_g) * (1.0 - lam0)
    return o.reshape(bsz, T, B_HEADS * B_V_DIM) @ w_o


def trunk(x, C0, n0, m0, past_k, past_v, past_len, norm_g, ffn_w_gate, ffn_w_up, ffn_w_down,
          a_w_in, a_b_if, a_head_g, a_w_out, kv_norm_g, w_kv, b_w_q, b_lambda, b_head_g, b_w_o):
    bsz, T, _ = x.shape
    new_C, new_n, new_m = [], [], []
    k_all = v_all = k_new = v_new = None
    for l in range(DEPTH):
        g = norm_g[l]
        x = x + 0.5 * rms_norm(swiglu(rms_norm(x, g[0]), ffn_w_gate[l, 0], ffn_w_up[l, 0], ffn_w_down[l, 0]), g[1])
        xn = rms_norm(x, g[2])
        if l < N_A_LAYERS:
            y, C, n, m = mlstm_mixer(xn, a_w_in[l], a_b_if[l], a_head_g[l], a_w_out[l], C0[l], n0[l], m0[l])
            new_C.append(C)
            new_n.append(n)
            new_m.append(m)
        else:
            j = l - N_A_LAYERS
            y = diff_attn_mixer(xn, b_w_q[j], b_lambda[j], b_head_g[j], b_w_o[j], k_all, v_all, l, past_len)
        x = x + rms_norm(y, g[3])
        x = x + 0.5 * rms_norm(swiglu(rms_norm(x, g[4]), ffn_w_gate[l, 1], ffn_w_up[l, 1], ffn_w_down[l, 1]), g[5])
        if l == N_A_LAYERS - 1:
            kv = rms_norm(x, kv_norm_g) @ w_kv
            k_new = kv[..., :B_HEADS * 2 * B_QK_DIM].reshape(bsz, T, B_HEADS, 2, B_QK_DIM)
            v_new = kv[..., B_HEADS * 2 * B_QK_DIM:].reshape(bsz, T, B_HEADS, B_V_DIM)
            if past_k is None:
                k_all, v_all = k_new, v_new
            else:
                k_all = jnp.concatenate([past_k.astype(k_new.dtype), k_new], axis=1)
                v_all = jnp.concatenate([past_v.astype(v_new.dtype), v_new], axis=1)
    return x, jnp.stack(new_C), jnp.stack(new_n), jnp.stack(new_m), k_new, v_new


def setup_inputs(seed: int = 0) -> dict:
    key = jax.random.key(seed)
    ks = jax.random.split(key, 24)
    f32 = jnp.float32

    def nrm(k, shape, scale):
        return jax.random.normal(k, shape, f32) * scale

    n_pages = PAST_LEN // PAGE_SIZE
    n_used = DEC_BATCH * n_pages
    n_pool = n_used + max(1, n_used // 4)
    perm = jax.random.permutation(ks[0], n_pool)
    page_table = perm[:n_used].reshape(DEC_BATCH, n_pages).astype(jnp.int32)
    b_if = jnp.concatenate([nrm(ks[1], (N_A_LAYERS, A_HEADS), 0.1),
                            3.0 + nrm(ks[2], (N_A_LAYERS, A_HEADS), 0.5)], axis=-1)
    return {
        "x_prompt": nrm(ks[3], (BATCH, SEQ, D_MODEL), 1.0),
        "x_sample": nrm(ks[4], (DEC_BATCH, DEC_SEQ, D_MODEL), 1.0),
        "state_C": nrm(ks[5], (N_A_LAYERS, DEC_BATCH, A_HEADS, A_HEAD_DIM, A_HEAD_DIM), 1.0),
        "state_n": jnp.abs(nrm(ks[6], (N_A_LAYERS, DEC_BATCH, A_HEADS, A_HEAD_DIM), 1.0)),
        "state_m": nrm(ks[7], (N_A_LAYERS, DEC_BATCH, A_HEADS), 1.0),
        "cache_k": nrm(ks[8], (n_pool, PAGE_SIZE, B_HEADS, 2, B_QK_DIM), 1.0),
        "cache_v": nrm(ks[9], (n_pool, PAGE_SIZE, B_HEADS, B_V_DIM), 1.0),
        "page_table": page_table,
        "norm_g": 1.0 + nrm(ks[10], (DEPTH, N_NORMS, D_MODEL), 0.05),
        "ffn_w_gate": nrm(ks[11], (DEPTH, 2, D_MODEL, D_FF), D_MODEL ** -0.5),
        "ffn_w_up": nrm(ks[12], (DEPTH, 2, D_MODEL, D_FF), D_MODEL ** -0.5),
        "ffn_w_down": nrm(ks[13], (DEPTH, 2, D_FF, D_MODEL), D_FF ** -0.5),
        "a_w_in": nrm(ks[14], (N_A_LAYERS, D_MODEL, A_PROJ), D_MODEL ** -0.5),
        "a_b_if": b_if,
        "a_head_g": 1.0 + nrm(ks[15], (N_A_LAYERS, A_HEADS, A_HEAD_DIM), 0.05),
        "a_w_out": nrm(ks[16], (N_A_LAYERS, D_MODEL, D_MODEL), D_MODEL ** -0.5),
        "kv_norm_g": 1.0 + nrm(ks[17], (D_MODEL,), 0.05),
        "w_kv": nrm(ks[18], (D_MODEL, KV_WIDTH), D_MODEL ** -0.5),
        "b_w_q": nrm(ks[19], (N_B_LAYERS, D_MODEL, B_HEADS * 2 * B_QK_DIM), D_MODEL ** -0.5),
        "b_lambda": nrm(ks[20], (N_B_LAYERS, 4, B_QK_DIM), 0.1),
        "b_head_g": 1.0 + nrm(ks[21], (N_B_LAYERS, B_HEADS, B_V_DIM), 0.05),
        "b_w_o": nrm(ks[22], (N_B_LAYERS, B_HEADS * B_V_DIM, D_MODEL), (B_HEADS * B_V_DIM) ** -0.5),
    }


def reference(x_prompt, x_sample, state_C, state_n, state_m, cache_k, cache_v, page_table,
              norm_g, ffn_w_gate, ffn_w_up, ffn_w_down, a_w_in, a_b_if, a_head_g, a_w_out,
              kv_norm_g, w_kv, b_w_q, b_lambda, b_head_g, b_w_o):
    bp = x_prompt.shape[0]
    zC = jnp.zeros((N_A_LAYERS, bp, A_HEADS, A_HEAD_DIM, A_HEAD_DIM), x_prompt.dtype)
    zn = jnp.zeros((N_A_LAYERS, bp, A_HEADS, A_HEAD_DIM), x_prompt.dtype)
    zm = jnp.zeros((N_A_LAYERS, bp, A_HEADS), x_prompt.dtype)
    y_prompt, C_p, n_p, m_p, k_p, v_p = trunk(
        x_prompt, zC, zn, zm, None, None, 0, norm_g, ffn_w_gate, ffn_w_up, ffn_w_down,
        a_w_in, a_b_if, a_head_g, a_w_out, kv_norm_g, w_kv, b_w_q, b_lambda, b_head_g, b_w_o)
    bd, npg = page_table.shape
    past_len = npg * PAGE_SIZE
    past_k = cache_k[page_table].reshape(bd, past_len, B_HEADS, 2, B_QK_DIM)
    past_v = cache_v[page_table].reshape(bd, past_len, B_HEADS, B_V_DIM)
    y_sample, C_s, n_s, m_s, k_s, v_s = trunk(
        x_sample, state_C, state_n, state_m, past_k, past_v, past_len, norm_g, ffn_w_gate, ffn_w_up,
        ffn_w_down, a_w_in, a_b_if, a_head_g, a_w_out, kv_norm_g, w_kv, b_w_q, b_lambda, b_head_g, b_w_o)
    return (y_prompt, y_sample, C_p, n_p, m_p, k_p, v_p, C_s, n_s, m_s, k_s, v_s)
```

```python
import functools
import math

import numpy as np
import jax
import jax.numpy as jnp
from jax import lax
from jax.experimental import pallas as pl
from jax.experimental.pallas import tpu as pltpu

EPS = 1e-6
BF16 = jnp.bfloat16
F32 = jnp.float32

V7X_LANES = 128
V7X_MXU_COLS = 256
V7X_VMEM_BYTES = 64 * 1024 * 1024
VMEM_LIMIT_BYTES = V7X_VMEM_BYTES - 6 * 1024 * 1024

SAMPLE_ROWS = 16
ROW_TILE = 1024
COL_TILE = V7X_MXU_COLS
NORM_ROWS = 256
MLSTM_CHUNK = 256
ATTN_Q_TILE = 256

_NT = (((1,), (1,)), ((), ()))
_TN = (((0,), (0,)), ((), ()))


def _params(semantics):
    return pltpu.CompilerParams(dimension_semantics=semantics, vmem_limit_bytes=VMEM_LIMIT_BYTES)


def _rms(x, g):
    return x * lax.rsqrt(jnp.mean(x * x, axis=-1, keepdims=True) + EPS) * g


def _sigmoid(x):
    return 1.0 / (1.0 + jnp.exp(-x))


def _rms_cast_kernel(x_ref, g_ref, o_ref):
    o_ref[...] = _rms(x_ref[...], g_ref[...]).astype(o_ref.dtype)


def _rms_cast(x, g):
    m, d = x.shape
    tr = min(NORM_ROWS, m)
    return pl.pallas_call(
        _rms_cast_kernel,
        out_shape=jax.ShapeDtypeStruct((m, d), BF16),
        grid=(m // tr,),
        in_specs=[pl.BlockSpec((tr, d), lambda i: (i, 0)),
                  pl.BlockSpec((1, d), lambda i: (0, 0))],
        out_specs=pl.BlockSpec((tr, d), lambda i: (i, 0)),
        compiler_params=_params(("parallel",)),
        name="rms_cast",
    )(x, g.reshape(1, d))


def _resid_norm_kernel(x_ref, y_ref, gp_ref, gn_ref, xo_ref, xn_ref, *, coef):
    x = x_ref[...] + coef * _rms(y_ref[...], gp_ref[...])
    xo_ref[...] = x
    xn_ref[...] = _rms(x, gn_ref[...]).astype(xn_ref.dtype)


def _resid_kernel(x_ref, y_ref, gp_ref, xo_ref, *, coef):
    xo_ref[...] = x_ref[...] + coef * _rms(y_ref[...], gp_ref[...])


def _resid_norm(x, y, g_post, g_next, coef):
    m, d = x.shape
    tr = min(NORM_ROWS, m)
    row = pl.BlockSpec((tr, d), lambda i: (i, 0))
    vec = pl.BlockSpec((1, d), lambda i: (0, 0))
    if g_next is None:
        return pl.pallas_call(
            functools.partial(_resid_kernel, coef=coef),
            out_shape=jax.ShapeDtypeStruct((m, d), F32),
            grid=(m // tr,),
            in_specs=[row, row, vec],
            out_specs=row,
            compiler_params=_params(("parallel",)),
            name="resid",
        )(x, y, g_post.reshape(1, d)), None
    return pl.pallas_call(
        functools.partial(_resid_norm_kernel, coef=coef),
        out_shape=(jax.ShapeDtypeStruct((m, d), F32), jax.ShapeDtypeStruct((m, d), BF16)),
        grid=(m // tr,),
        in_specs=[row, row, vec, vec],
        out_specs=(row, row),
        compiler_params=_params(("parallel",)),
        name="resid_norm",
    )(x, y, g_post.reshape(1, d), g_next.reshape(1, d))


def _mm_kernel(x_ref, w_ref, o_ref):
    w = w_ref[...].astype(BF16)
    o_ref[...] = jnp.dot(x_ref[...], w, preferred_element_type=F32).astype(o_ref.dtype)


def _matmul(x, w, lead, n0, n, out_dtype, name):
    m, k = x.shape
    tm = min(ROW_TILE, m)
    tn = min(COL_TILE, n)
    assert m % tm == 0 and n % tn == 0 and n0 % tn == 0
    jb0 = n0 // tn
    nlead = len(lead)
    x_mode = dict(pipeline_mode=pl.Buffered(1)) if tm * k * 2 > (8 << 20) else {}
    return pl.pallas_call(
        _mm_kernel,
        out_shape=jax.ShapeDtypeStruct((m, n), out_dtype),
        grid=(m // tm, n // tn),
        in_specs=[pl.BlockSpec((tm, k), lambda i, j: (i, 0), **x_mode),
                  pl.BlockSpec((None,) * nlead + (k, tn), lambda i, j: lead + (0, j + jb0))],
        out_specs=pl.BlockSpec((tm, tn), lambda i, j: (i, j)),
        compiler_params=_params(("parallel", "arbitrary")),
        name=name,
    )(x, w)


def _gate_up_kernel(x_ref, wg_ref, wu_ref, o_ref):
    x = x_ref[...]
    g = jnp.dot(x, wg_ref[...].astype(BF16), preferred_element_type=F32)
    u = jnp.dot(x, wu_ref[...].astype(BF16), preferred_element_type=F32)
    o_ref[...] = (g * _sigmoid(g) * u).astype(o_ref.dtype)


def _gate_up(x, w_gate, w_up, lead):
    m, k = x.shape
    f = w_gate.shape[-1]
    tm = min(ROW_TILE, m)
    tn = COL_TILE
    assert m % tm == 0 and f % tn == 0
    w_spec = pl.BlockSpec((None, None, k, tn), lambda i, j: lead + (0, j))
    return pl.pallas_call(
        _gate_up_kernel,
        out_shape=jax.ShapeDtypeStruct((m, f), BF16),
        grid=(m // tm, f // tn),
        in_specs=[pl.BlockSpec((tm, k), lambda i, j: (i, 0)), w_spec, w_spec],
        out_specs=pl.BlockSpec((tm, tn), lambda i, j: (i, j)),
        compiler_params=_params(("parallel", "arbitrary")),
        name="ffn_gate_up",
    )(x, w_gate, w_up)


def _mlstm_kernel(*refs, chunk, valid, heads, has_state, scale):
    if has_state:
        (q_ref, k_ref, v_ref, o_ref, gc_ref, bias_ref, hg_ref, c0_ref, n0_ref, m0_ref,
         h_ref, c_ref, n_ref, m_ref) = refs
    else:
        (q_ref, k_ref, v_ref, o_ref, gc_ref, bias_ref, hg_ref,
         h_ref, c_ref, n_ref, m_ref) = refs
    hh = pl.program_id(1)
    cc = pl.program_id(2)
    L = chunk

    @pl.when(cc == 0)
    def _init():
        if has_state:
            c_ref[...] = c0_ref[...]
            n_ref[...] = n0_ref[...]
            m_ref[...] = m0_ref[...]
        else:
            c_ref[...] = jnp.zeros_like(c_ref)
            n_ref[...] = jnp.zeros_like(n_ref)
            m_ref[...] = jnp.zeros_like(m_ref)

    gc = gc_ref[...] + bias_ref[...]
    lane = lax.broadcasted_iota(jnp.int32, gc.shape, 1)
    li_col = jnp.sum(jnp.where(lane == hh, gc, 0.0), axis=1, keepdims=True)
    f_col = jnp.sum(jnp.where(lane == hh + heads, gc, 0.0), axis=1, keepdims=True)
    lf_col = jnp.minimum(f_col, 0.0) - jnp.log(1.0 + jnp.exp(-jnp.abs(f_col)))
    if valid < L:
        rid = lax.broadcasted_iota(jnp.int32, (L, 1), 0)
        li_col = jnp.where(rid < valid, li_col, -jnp.inf)
        lf_col = jnp.where(rid < valid, lf_col, 0.0)

    r = lax.broadcasted_iota(jnp.int32, (L, L), 0)
    s = lax.broadcasted_iota(jnp.int32, (L, L), 1)
    causal = s <= r
    li_row = jnp.sum(jnp.where(r == s, li_col, 0.0), axis=0, keepdims=True)
    lf_row = jnp.sum(jnp.where(r == s, lf_col, 0.0), axis=0, keepdims=True)
    b_col = jnp.sum(jnp.where(causal, lf_row, 0.0), axis=1, keepdims=True)
    b_row = jnp.sum(jnp.where(r <= s, lf_col, 0.0), axis=0, keepdims=True)

    m_prev = m_ref[...]
    m_inter = b_col + m_prev
    log_d = jnp.where(causal, b_col - b_row + li_row, -jnp.inf)
    m_t = jnp.maximum(m_inter, jnp.max(log_d, axis=1, keepdims=True))
    decay_mat = jnp.exp(log_d - m_t)
    inter = jnp.exp(m_inter - m_t)

    q = q_ref[...]
    k = k_ref[...]
    v = v_ref[...]
    c_state = c_ref[...]
    n_state = n_ref[...]
    smat = lax.dot_general(q, k, _NT, preferred_element_type=F32) * scale * decay_mat
    q_c = lax.dot_general(q, c_state.astype(BF16), _NT, preferred_element_type=F32)
    num = jnp.dot(smat.astype(BF16), v, preferred_element_type=F32) + inter * q_c
    q_n = jnp.sum(q.astype(F32) * n_state, axis=1, keepdims=True)
    den = jnp.sum(smat, axis=1, keepdims=True) + inter * q_n
    hval = num / jnp.maximum(jnp.abs(den), jnp.exp(-m_t))

    hg = hg_ref[pl.ds(hh, 1), :]
    h_ref[...] = (_rms(hval, hg) * _sigmoid(o_ref[...].astype(F32))).astype(h_ref.dtype)

    b_last = jnp.sum(lf_col, axis=0, keepdims=True)
    log_w = b_last - b_col + li_col
    m_new = jnp.maximum(b_last + m_prev, jnp.max(log_w, axis=0, keepdims=True))
    w_col = jnp.exp(log_w - m_new)
    decay = jnp.exp(b_last + m_prev - m_new)
    k_scaled = k.astype(F32) * scale
    v_w = (v.astype(F32) * w_col).astype(BF16)
    c_ref[...] = decay * c_state + lax.dot_general(v_w, k_scaled.astype(BF16), _TN,
                                                   preferred_element_type=F32)
    n_ref[...] = decay * n_state + jnp.sum(k_scaled * w_col, axis=0, keepdims=True)
    m_ref[...] = m_new


def _mlstm(proj, gates, bias, head_g, state, bsz, seq, valid, chunk):
    heads, dh = head_g.shape
    d = heads * dh
    nc = seq // chunk
    assert seq % chunk == 0
    has_state = state is not None

    def col(part):
        return pl.BlockSpec((chunk, dh), lambda b, h, c: (b * nc + c, part * heads + h))

    c_spec = pl.BlockSpec((None, None, dh, dh), lambda b, h, c: (b, h, 0, 0))
    n_spec = pl.BlockSpec((None, None, 1, dh), lambda b, h, c: (b, h, 0, 0))
    m_spec = pl.BlockSpec((None, None, 1, 1), lambda b, h, c: (b, h, 0, 0))
    in_specs = [col(0), col(1), col(2), col(3),
                pl.BlockSpec((chunk, V7X_LANES), lambda b, h, c: (b * nc + c, 0)),
                pl.BlockSpec((1, V7X_LANES), lambda b, h, c: (0, 0)),
                pl.BlockSpec((heads, dh), lambda b, h, c: (0, 0))]
    args = [proj, proj, proj, proj, gates, bias, head_g]
    if has_state:
        c0, n0, m0 = state
        in_specs += [c_spec, n_spec, m_spec]
        args += [c0, n0.reshape(bsz, heads, 1, dh), m0.reshape(bsz, heads, 1, 1)]
    h, c_new, n_new, m_new = pl.pallas_call(
        functools.partial(_mlstm_kernel, chunk=chunk, valid=valid, heads=heads,
                          has_state=has_state, scale=dh ** -0.5),
        out_shape=(jax.ShapeDtypeStruct((bsz * seq, d), BF16),
                   jax.ShapeDtypeStruct((bsz, heads, dh, dh), F32),
                   jax.ShapeDtypeStruct((bsz, heads, 1, dh), F32),
                   jax.ShapeDtypeStruct((bsz, heads, 1, 1), F32)),
        grid=(bsz, heads, nc),
        in_specs=in_specs,
        out_specs=(pl.BlockSpec((chunk, dh), lambda b, h, c: (b * nc + c, h)),
                   c_spec, n_spec, m_spec),
        compiler_params=_params(("parallel", "parallel", "arbitrary")),
        name="mlstm",
    )(*args)
    return h, c_new, n_new.reshape(bsz, heads, dh), m_new.reshape(bsz, heads)


def _lambda(lam_ref, lam0):
    lp = lam_ref[...]
    return (jnp.exp(jnp.sum(lp[0:1] * lp[1:2], axis=1, keepdims=True))
            - jnp.exp(jnp.sum(lp[2:3] * lp[3:4], axis=1, keepdims=True)) + lam0)


def _alibi_slopes(heads):
    return np.asarray(2.0 ** (-8.0 * np.arange(1, heads + 1) / heads), np.float32)


def _dattn_prompt_kernel(slopes_ref, q_ref, k_ref, v_ref, lam_ref, hg_ref, o_ref, *, tq, dk, lam0):
    hh = pl.program_id(1)
    qi = pl.program_id(2)
    lam = _lambda(lam_ref, lam0)
    slope = slopes_ref[hh]
    q = q_ref[...]
    kb = k_ref[...].astype(BF16)
    vb = v_ref[...].astype(BF16)
    seq = kb.shape[0]
    dist = (qi * tq + lax.broadcasted_iota(jnp.int32, (tq, seq), 0)
            - lax.broadcasted_iota(jnp.int32, (tq, seq), 1))
    keep = dist >= 0
    bias = slope * dist.astype(F32)
    probs = []
    for j in range(2):
        sc = lax.dot_general(q[:, j * dk:(j + 1) * dk], kb[:, j * dk:(j + 1) * dk], _NT,
                             preferred_element_type=F32) * (dk ** -0.5)
        sc = jnp.where(keep, sc - bias, -jnp.inf)
        e = jnp.exp(sc - jnp.max(sc, axis=1, keepdims=True))
        probs.append(e / jnp.sum(e, axis=1, keepdims=True))
    a = (probs[0] - lam * probs[1]).astype(BF16)
    o = jnp.dot(a, vb, preferred_element_type=F32)
    hg = hg_ref[pl.ds(hh, 1), :]
    o_ref[...] = (_rms(o, hg) * (1.0 - lam0)).astype(o_ref.dtype)


def _dattn_prompt(q, k, v, lam_p, head_g, bsz, seq, lam0):
    heads, dv = head_g.shape
    dk = dv // 2
    tq = min(ATTN_Q_TILE, seq)
    nq = seq // tq
    kv_spec = pl.BlockSpec((seq, dv), lambda b, h, i: (b, h))
    q_spec = pl.BlockSpec((tq, dv), lambda b, h, i: (b * nq + i, h))
    return pl.pallas_call(
        functools.partial(_dattn_prompt_kernel, tq=tq, dk=dk, lam0=lam0),
        out_shape=jax.ShapeDtypeStruct(q.shape, BF16),
        grid=(bsz, heads, nq),
        in_specs=[pl.BlockSpec(memory_space=pltpu.SMEM),
                  q_spec, kv_spec, kv_spec,
                  pl.BlockSpec((4, dk), lambda b, h, i: (0, 0)),
                  pl.BlockSpec((heads, dv), lambda b, h, i: (0, 0))],
        out_specs=q_spec,
        compiler_params=_params(("parallel", "parallel", "arbitrary")),
        name="dattn_prompt",
    )(jnp.asarray(_alibi_slopes(heads)), q, k, v, lam_p, head_g)


def _dattn_sample_kernel(pt_ref, q_ref, kc_ref, vc_ref, kn_ref, vn_ref, slope_ref, lam_ref, hg_ref,
                         o_ref, m_scr, l_scr, acc_scr, *, heads, dk, rows, page, n_pages, lam0):
    pg = pl.program_id(1)
    dv = 2 * dk
    past_len = n_pages * page
    n_rows = heads * 2 * rows

    @pl.when(pg == 0)
    def _init():
        m_scr[...] = jnp.full_like(m_scr, -jnp.inf)
        l_scr[...] = jnp.zeros_like(l_scr)
        acc_scr[...] = jnp.zeros_like(acc_scr)

    def process(k_ref, v_ref, kpos0):
        q = q_ref[...]
        kb = k_ref[...].astype(BF16)
        vb = v_ref[...].astype(BF16)
        pieces = []
        for hj in range(heads * 2):
            pieces.append(lax.dot_general(q[:, hj * dk:(hj + 1) * dk], kb[:, hj * dk:(hj + 1) * dk],
                                          _NT, preferred_element_type=F32))
        sc = jnp.concatenate(pieces, axis=0) * (dk ** -0.5)
        t_idx = lax.broadcasted_iota(jnp.int32, (n_rows, page), 0) % rows
        dist = past_len + t_idx - (kpos0 + lax.broadcasted_iota(jnp.int32, (n_rows, page), 1))
        sc = jnp.where(dist >= 0, sc - slope_ref[...] * dist.astype(F32), -jnp.inf)
        m_old = m_scr[...]
        m_new = jnp.maximum(m_old, jnp.max(sc, axis=1, keepdims=True))
        alpha = jnp.exp(m_old - m_new)
        p = jnp.exp(sc - m_new)
        l_scr[...] = alpha * l_scr[...] + jnp.sum(p, axis=1, keepdims=True)
        m_scr[...] = m_new
        for hj in range(heads * 2):
            h = hj // 2
            r0 = hj * rows
            pv = jnp.dot(p[r0:r0 + rows, :].astype(BF16), vb[:, h * dv:(h + 1) * dv],
                         preferred_element_type=F32)
            acc_scr[r0:r0 + rows, :] = alpha[r0:r0 + rows, :] * acc_scr[r0:r0 + rows, :] + pv

    @pl.when(pg < n_pages)
    def _past():
        process(kc_ref, vc_ref, pg * page)

    @pl.when(pg == n_pages)
    def _new():
        process(kn_ref, vn_ref, past_len)
        lam = _lambda(lam_ref, lam0)
        on = acc_scr[...] / l_scr[...]
        for h in range(heads):
            r0 = 2 * h * rows
            o = on[r0:r0 + rows, :] - lam * on[r0 + rows:r0 + 2 * rows, :]
            o_ref[:, h * dv:(h + 1) * dv] = (_rms(o, hg_ref[h:h + 1, :]) * (1.0 - lam0)).astype(o_ref.dtype)


def _dattn_sample(q, cache_k, cache_v, page_table, k_new, v_new, lam_p, head_g, rows, lam0):
    heads, dv = head_g.shape
    dk = dv // 2
    bd, n_pages = page_table.shape
    _, page, hw = cache_k.shape
    n_rows = heads * 2 * rows
    slope_rows = jnp.asarray(np.repeat(_alibi_slopes(heads), 2 * rows).reshape(n_rows, 1))

    def page_map(b, p, pt):
        return (pt[b * n_pages + jnp.minimum(p, n_pages - 1)], 0, 0)

    cache_spec = pl.BlockSpec((None, page, hw), page_map)
    new_spec = pl.BlockSpec((None, page, hw), lambda b, p, pt: (b, 0, 0))
    q_spec = pl.BlockSpec((rows, hw), lambda b, p, pt: (b, 0))
    return pl.pallas_call(
        functools.partial(_dattn_sample_kernel, heads=heads, dk=dk, rows=rows, page=page,
                          n_pages=n_pages, lam0=lam0),
        out_shape=jax.ShapeDtypeStruct(q.shape, BF16),
        grid_spec=pltpu.PrefetchScalarGridSpec(
            num_scalar_prefetch=1,
            grid=(bd, n_pages + 1),
            in_specs=[q_spec, cache_spec, cache_spec, new_spec, new_spec,
                      pl.BlockSpec((n_rows, 1), lambda b, p, pt: (0, 0)),
                      pl.BlockSpec((4, dk), lambda b, p, pt: (0, 0)),
                      pl.BlockSpec((heads, dv), lambda b, p, pt: (0, 0))],
            out_specs=q_spec,
            scratch_shapes=[pltpu.VMEM((n_rows, 1), F32), pltpu.VMEM((n_rows, 1), F32),
                            pltpu.VMEM((n_rows, dv), F32)]),
        compiler_params=_params(("parallel", "arbitrary")),
        name="dattn_sample",
    )(page_table.reshape(-1), q, cache_k, cache_v, k_new, v_new, slope_rows, lam_p, head_g)


def _lambda_init(layer):
    return 0.8 - 0.6 * math.exp(-0.3 * layer)


def _trunk(x, bsz, seq, valid, state, past, p):
    depth = p["norm_g"].shape[0]
    n_a = p["a_w_in"].shape[0]
    d = x.shape[1]
    heads_a = p["a_head_g"].shape[1]
    hw_k = p["b_w_q"].shape[2]
    chunk = min(MLSTM_CHUNK, seq)
    new_c, new_n, new_m = [], [], []
    k_new = v_new = None
    xn = _rms_cast(x, p["norm_g"][0, 0])
    for l in range(depth):
        g = p["norm_g"][l]
        h = _gate_up(xn, p["ffn_w_gate"], p["ffn_w_up"], (l, 0))
        y = _matmul(h, p["ffn_w_down"], (l, 0), 0, d, F32, "ffn_down")
        x, xn = _resid_norm(x, y, g[1], g[2], 0.5)
        if l < n_a:
            w_in = p["a_w_in"]
            proj = _matmul(xn, w_in, (l,), 0, 4 * d, BF16, "mlstm_in")
            w_gates = jnp.pad(w_in[l, :, 4 * d:], ((0, 0), (0, V7X_LANES - 2 * heads_a)))
            gates = _matmul(xn, w_gates, (), 0, V7X_LANES, F32, "mlstm_gates")
            bias = jnp.pad(p["a_b_if"][l], (0, V7X_LANES - 2 * heads_a)).reshape(1, V7X_LANES)
            st = None if state is None else (state[0][l], state[1][l], state[2][l])
            hmix, c_l, n_l, m_l = _mlstm(proj, gates, bias, p["a_head_g"][l], st, bsz, seq, valid, chunk)
            new_c.append(c_l)
            new_n.append(n_l)
            new_m.append(m_l)
            y = _matmul(hmix, p["a_w_out"], (l,), 0, d, F32, "mlstm_out")
        else:
            j = l - n_a
            q = _matmul(xn, p["b_w_q"], (j,), 0, hw_k, BF16, "dattn_q")
            lam0 = _lambda_init(l)
            if past is None:
                o = _dattn_prompt(q, k_new, v_new, p["b_lambda"][j], p["b_head_g"][j], bsz, seq, lam0)
            else:
                cache_k, cache_v, page_table, kn_pad, vn_pad = past
                o = _dattn_sample(q, cache_k, cache_v, page_table, kn_pad, vn_pad,
                                  p["b_lambda"][j], p["b_head_g"][j], seq, lam0)
            y = _matmul(o, p["b_w_o"], (j,), 0, d, F32, "dattn_o")
        x, xn = _resid_norm(x, y, g[3], g[4], 1.0)
        h = _gate_up(xn, p["ffn_w_gate"], p["ffn_w_up"], (l, 1))
        y = _matmul(h, p["ffn_w_down"], (l, 1), 0, d, F32, "ffn_down")
        g_next = p["norm_g"][l + 1, 0] if l + 1 < depth else None
        x, xn_next = _resid_norm(x, y, g[5], g_next, 0.5)
        if l == n_a - 1:
            xkv = _rms_cast(x, p["kv_norm_g"])
            k_new = _matmul(xkv, p["w_kv"], (), 0, hw_k, F32, "kv_k")
            v_new = _matmul(xkv, p["w_kv"], (), hw_k, p["w_kv"].shape[1] - hw_k, F32, "kv_v")
            if past is not None:
                cache_k, cache_v, page_table = past
                page = cache_k.shape[1]
                pad = ((0, 0), (0, page - seq), (0, 0))
                past = (cache_k, cache_v, page_table,
                        jnp.pad(k_new.reshape(bsz, seq, hw_k), pad),
                        jnp.pad(v_new.reshape(bsz, seq, hw_k), pad))
        xn = xn_next
    return x, jnp.stack(new_c), jnp.stack(new_n), jnp.stack(new_m), k_new, v_new


def kernel(x_prompt, x_sample, state_C, state_n, state_m, cache_k, cache_v, page_table, norm_g,
           ffn_w_gate, ffn_w_up, ffn_w_down, a_w_in, a_b_if, a_head_g, a_w_out, kv_norm_g, w_kv,
           b_w_q, b_lambda, b_head_g, b_w_o):
    p = dict(norm_g=norm_g, ffn_w_gate=ffn_w_gate, ffn_w_up=ffn_w_up, ffn_w_down=ffn_w_down,
             a_w_in=a_w_in, a_b_if=a_b_if, a_head_g=a_head_g, a_w_out=a_w_out, kv_norm_g=kv_norm_g,
             w_kv=w_kv, b_w_q=b_w_q, b_lambda=b_lambda, b_head_g=b_head_g, b_w_o=b_w_o)
    bp, tp, d = x_prompt.shape
    bd, td, _ = x_sample.shape
    heads_b, dv = b_head_g.shape[1:]
    dk = dv // 2
    n_pool, page = cache_k.shape[:2]

    y_p, c_p, n_p, m_p, k_p, v_p = _trunk(x_prompt.reshape(bp * tp, d), bp, tp, tp, None, None, p)

    assert td <= SAMPLE_ROWS
    xs = jnp.pad(x_sample, ((0, 0), (0, SAMPLE_ROWS - td), (0, 0))).reshape(bd * SAMPLE_ROWS, d)
    past = (cache_k.reshape(n_pool, page, -1), cache_v.reshape(n_pool, page, -1), page_table)
    y_s, c_s, n_s, m_s, k_s, v_s = _trunk(xs, bd, SAMPLE_ROWS, td, (state_C, state_n, state_m), past, p)

    def unpad(a, shape):
        return a.reshape(bd, SAMPLE_ROWS, -1)[:, :td].reshape(shape)

    return (y_p.reshape(bp, tp, d), unpad(y_s, (bd, td, d)), c_p, n_p, m_p,
            k_p.reshape(bp, tp, heads_b, 2, dk), v_p.reshape(bp, tp, heads_b, dv),
            c_s, n_s, m_s,
            unpad(k_s, (bd, td, heads_b, 2, dk)), unpad(v_s, (bd, td, heads_b, dv)))
```

```python
import functools
import math

import numpy as np
import jax
import jax.numpy as jnp
from jax import lax
from jax.experimental import pallas as pl
from jax.experimental.pallas import tpu as pltpu

EPS = 1e-6
BF16 = jnp.bfloat16
F32 = jnp.float32

V7X_LANES = 128
V7X_MXU_COLS = 256
V7X_VMEM_BYTES = 64 * 1024 * 1024
VMEM_LIMIT_BYTES = V7X_VMEM_BYTES - 6 * 1024 * 1024

SAMPLE_ROWS = 16
ROW_TILE = 1024
COL_TILE = V7X_MXU_COLS
NORM_ROWS = 256
MLSTM_CHUNK = 256
ATTN_Q_TILE = 256

_NT = (((1,), (1,)), ((), ()))
_TN = (((0,), (0,)), ((), ()))


def _params(semantics):
    return pltpu.CompilerParams(dimension_semantics=semantics, vmem_limit_bytes=VMEM_LIMIT_BYTES)


def _rms(x, g):
    return x * lax.rsqrt(jnp.mean(x * x, axis=-1, keepdims=True) + EPS) * g


def _sigmoid(x):
    return 1.0 / (1.0 + jnp.exp(-x))


def _rms_cast_kernel(x_ref, g_ref, o_ref):
    o_ref[...] = _rms(x_ref[...], g_ref[...]).astype(o_ref.dtype)


def _rms_cast(x, g):
    m, d = x.shape
    tr = min(NORM_ROWS, m)
    return pl.pallas_call(
        _rms_cast_kernel,
        out_shape=jax.ShapeDtypeStruct((m, d), BF16),
        grid=(m // tr,),
        in_specs=[pl.BlockSpec((tr, d), lambda i: (i, 0)),
                  pl.BlockSpec((1, d), lambda i: (0, 0))],
        out_specs=pl.BlockSpec((tr, d), lambda i: (i, 0)),
        compiler_params=_params(("parallel",)),
        name="rms_cast",
    )(x, g.reshape(1, d))


def _resid_norm_kernel(x_ref, y_ref, gp_ref, gn_ref, xo_ref, xn_ref, *, coef):
    x = x_ref[...] + coef * _rms(y_ref[...], gp_ref[...])
    xo_ref[...] = x
    xn_ref[...] = _rms(x, gn_ref[...]).astype(xn_ref.dtype)


def _resid_kernel(x_ref, y_ref, gp_ref, xo_ref, *, coef):
    xo_ref[...] = x_ref[...] + coef * _rms(y_ref[...], gp_ref[...])


def _resid_norm(x, y, g_post, g_next, coef):
    m, d = x.shape
    tr = min(NORM_ROWS, m)
    row = pl.BlockSpec((tr, d), lambda i: (i, 0))
    vec = pl.BlockSpec((1, d), lambda i: (0, 0))
    if g_next is None:
        return pl.pallas_call(
            functools.partial(_resid_kernel, coef=coef),
            out_shape=jax.ShapeDtypeStruct((m, d), F32),
            grid=(m // tr,),
            in_specs=[row, row, vec],
            out_specs=row,
            compiler_params=_params(("parallel",)),
            name="resid",
        )(x, y, g_post.reshape(1, d)), None
    return pl.pallas_call(
        functools.partial(_resid_norm_kernel, coef=coef),
        out_shape=(jax.ShapeDtypeStruct((m, d), F32), jax.ShapeDtypeStruct((m, d), BF16)),
        grid=(m // tr,),
        in_specs=[row, row, vec, vec],
        out_specs=(row, row),
        compiler_params=_params(("parallel",)),
        name="resid_norm",
    )(x, y, g_post.reshape(1, d), g_next.reshape(1, d))


def _sample_rows(first_row_tile, compute, os_ref):
    @pl.when(first_row_tile)
    def _():
        os_ref[...] = compute().astype(os_ref.dtype)

    @pl.when(jnp.logical_not(first_row_tile))
    def _():
        os_ref[...] = jnp.zeros_like(os_ref)


def _mm_kernel(x_ref, xs_ref, w_ref, o_ref, os_ref):
    w = w_ref[...].astype(BF16)
    o_ref[...] = jnp.dot(x_ref[...], w, preferred_element_type=F32).astype(o_ref.dtype)
    _sample_rows(pl.program_id(0) == 0,
                 lambda: jnp.dot(xs_ref[...], w, preferred_element_type=F32), os_ref)


def _col_tile(k, n):
    tn = min(COL_TILE, n)
    while n % (2 * tn) == 0 and k * 2 * tn * 4 <= (8 << 20):
        tn *= 2
    return tn


def _matmul(x, xs, w, lead, n0, n, out_dtype, name):
    m, k = x.shape
    ms = xs.shape[0]
    tm = min(ROW_TILE, m)
    tn = _col_tile(k, n)
    assert m % tm == 0 and n % tn == 0 and n0 % tn == 0
    jb0 = n0 // tn
    nlead = len(lead)
    x_mode = dict(pipeline_mode=pl.Buffered(1)) if tm * k * 2 > (8 << 20) else {}
    y, ys = pl.pallas_call(
        _mm_kernel,
        out_shape=(jax.ShapeDtypeStruct((m, n), out_dtype), jax.ShapeDtypeStruct((2, ms, n), out_dtype)),
        grid=(m // tm, n // tn),
        in_specs=[pl.BlockSpec((tm, k), lambda i, j: (i, 0), **x_mode),
                  pl.BlockSpec((ms, k), lambda i, j: (0, 0), **x_mode),
                  pl.BlockSpec((None,) * nlead + (k, tn), lambda i, j: lead + (0, j + jb0))],
        out_specs=(pl.BlockSpec((tm, tn), lambda i, j: (i, j)),
                   pl.BlockSpec((None, ms, tn), lambda i, j: (jnp.minimum(i, 1), 0, j))),
        compiler_params=_params(("arbitrary", "arbitrary")),
        name=name,
    )(x, xs, w)
    return y, ys[0]


def _gate_up_kernel(x_ref, xs_ref, wg_ref, wu_ref, o_ref, os_ref):
    wg = wg_ref[...].astype(BF16)
    wu = wu_ref[...].astype(BF16)

    def swiglu(x):
        g = jnp.dot(x, wg, preferred_element_type=F32)
        u = jnp.dot(x, wu, preferred_element_type=F32)
        return g * _sigmoid(g) * u

    o_ref[...] = swiglu(x_ref[...]).astype(o_ref.dtype)
    _sample_rows(pl.program_id(0) == 0, lambda: swiglu(xs_ref[...]), os_ref)


def _gate_up(x, xs, w_gate, w_up, lead):
    m, k = x.shape
    ms = xs.shape[0]
    f = w_gate.shape[-1]
    tm = min(ROW_TILE, m)
    tn = COL_TILE
    assert m % tm == 0 and f % tn == 0
    w_spec = pl.BlockSpec((None, None, k, tn), lambda i, j: lead + (0, j))
    h, hs = pl.pallas_call(
        _gate_up_kernel,
        out_shape=(jax.ShapeDtypeStruct((m, f), BF16), jax.ShapeDtypeStruct((2, ms, f), BF16)),
        grid=(m // tm, f // tn),
        in_specs=[pl.BlockSpec((tm, k), lambda i, j: (i, 0)),
                  pl.BlockSpec((ms, k), lambda i, j: (0, 0)), w_spec, w_spec],
        out_specs=(pl.BlockSpec((tm, tn), lambda i, j: (i, j)),
                   pl.BlockSpec((None, ms, tn), lambda i, j: (jnp.minimum(i, 1), 0, j))),
        compiler_params=_params(("arbitrary", "arbitrary")),
        name="ffn_gate_up",
    )(x, xs, w_gate, w_up)
    return h, hs[0]


def _mlstm_kernel(*refs, chunk, valid, heads, has_state, scale):
    if has_state:
        (q_ref, k_ref, v_ref, o_ref, gc_ref, bias_ref, hg_ref, c0_ref, n0_ref, m0_ref,
         h_ref, c_ref, n_ref, m_ref) = refs
    else:
        (q_ref, k_ref, v_ref, o_ref, gc_ref, bias_ref, hg_ref,
         h_ref, c_ref, n_ref, m_ref) = refs
    hh = pl.program_id(1)
    cc = pl.program_id(2)
    L = chunk

    @pl.when(cc == 0)
    def _init():
        if has_state:
            c_ref[...] = c0_ref[...]
            n_ref[...] = n0_ref[...]
            m_ref[...] = m0_ref[...]
        else:
            c_ref[...] = jnp.zeros_like(c_ref)
            n_ref[...] = jnp.zeros_like(n_ref)
            m_ref[...] = jnp.zeros_like(m_ref)

    gc = gc_ref[...] + bias_ref[...]
    lane = lax.broadcasted_iota(jnp.int32, gc.shape, 1)
    li_col = jnp.sum(jnp.where(lane == hh, gc, 0.0), axis=1, keepdims=True)
    f_col = jnp.sum(jnp.where(lane == hh + heads, gc, 0.0), axis=1, keepdims=True)
    lf_col = jnp.minimum(f_col, 0.0) - jnp.log(1.0 + jnp.exp(-jnp.abs(f_col)))
    if valid < L:
        rid = lax.broadcasted_iota(jnp.int32, (L, 1), 0)
        li_col = jnp.where(rid < valid, li_col, -jnp.inf)
        lf_col = jnp.where(rid < valid, lf_col, 0.0)

    r = lax.broadcasted_iota(jnp.int32, (L, L), 0)
    s = lax.broadcasted_iota(jnp.int32, (L, L), 1)
    causal = s <= r
    li_row = jnp.sum(jnp.where(r == s, li_col, 0.0), axis=0, keepdims=True)
    lf_row = jnp.sum(jnp.where(r == s, lf_col, 0.0), axis=0, keepdims=True)
    b_col = jnp.sum(jnp.where(causal, lf_row, 0.0), axis=1, keepdims=True)
    b_row = jnp.sum(jnp.where(r <= s, lf_col, 0.0), axis=0, keepdims=True)

    m_prev = m_ref[...]
    m_inter = b_col + m_prev
    log_d = jnp.where(causal, b_col - b_row + li_row, -jnp.inf)
    m_t = jnp.maximum(m_inter, jnp.max(log_d, axis=1, keepdims=True))
    decay_mat = jnp.exp(log_d - m_t)
    inter = jnp.exp(m_inter - m_t)

    q = q_ref[...]
    k = k_ref[...]
    v = v_ref[...]
    c_state = c_ref[...]
    n_state = n_ref[...]
    smat = lax.dot_general(q, k, _NT, preferred_element_type=F32) * scale * decay_mat
    q_c = lax.dot_general(q, c_state.astype(BF16), _NT, preferred_element_type=F32)
    num = jnp.dot(smat.astype(BF16), v, preferred_element_type=F32) + inter * q_c
    q_n = jnp.sum(q.astype(F32) * n_state, axis=1, keepdims=True)
    den = jnp.sum(smat, axis=1, keepdims=True) + inter * q_n
    hval = num / jnp.maximum(jnp.abs(den), jnp.exp(-m_t))

    hg = hg_ref[pl.ds(hh, 1), :]
    h_ref[...] = (_rms(hval, hg) * _sigmoid(o_ref[...].astype(F32))).astype(h_ref.dtype)

    b_last = jnp.sum(lf_col, axis=0, keepdims=True)
    log_w = b_last - b_col + li_col
    m_new = jnp.maximum(b_last + m_prev, jnp.max(log_w, axis=0, keepdims=True))
    w_col = jnp.exp(log_w - m_new)
    decay = jnp.exp(b_last + m_prev - m_new)
    k_scaled = k.astype(F32) * scale
    v_w = (v.astype(F32) * w_col).astype(BF16)
    c_ref[...] = decay * c_state + lax.dot_general(v_w, k_scaled.astype(BF16), _TN,
                                                   preferred_element_type=F32)
    n_ref[...] = decay * n_state + jnp.sum(k_scaled * w_col, axis=0, keepdims=True)
    m_ref[...] = m_new


def _mlstm(proj, gates, bias, head_g, state, bsz, seq, valid, chunk):
    heads, dh = head_g.shape
    d = heads * dh
    nc = seq // chunk
    assert seq % chunk == 0
    has_state = state is not None

    def col(part):
        return pl.BlockSpec((chunk, dh), lambda b, h, c: (b * nc + c, part * heads + h))

    c_spec = pl.BlockSpec((None, None, dh, dh), lambda b, h, c: (b, h, 0, 0))
    n_spec = pl.BlockSpec((None, None, 1, dh), lambda b, h, c: (b, h, 0, 0))
    m_spec = pl.BlockSpec((None, None, 1, 1), lambda b, h, c: (b, h, 0, 0))
    in_specs = [col(0), col(1), col(2), col(3),
                pl.BlockSpec((chunk, V7X_LANES), lambda b, h, c: (b * nc + c, 0)),
                pl.BlockSpec((1, V7X_LANES), lambda b, h, c: (0, 0)),
                pl.BlockSpec((heads, dh), lambda b, h, c: (0, 0))]
    args = [proj, proj, proj, proj, gates, bias, head_g]
    if has_state:
        c0, n0, m0 = state
        in_specs += [c_spec, n_spec, m_spec]
        args += [c0, n0.reshape(bsz, heads, 1, dh), m0.reshape(bsz, heads, 1, 1)]
    h, c_new, n_new, m_new = pl.pallas_call(
        functools.partial(_mlstm_kernel, chunk=chunk, valid=valid, heads=heads,
                          has_state=has_state, scale=dh ** -0.5),
        out_shape=(jax.ShapeDtypeStruct((bsz * seq, d), BF16),
                   jax.ShapeDtypeStruct((bsz, heads, dh, dh), F32),
                   jax.ShapeDtypeStruct((bsz, heads, 1, dh), F32),
                   jax.ShapeDtypeStruct((bsz, heads, 1, 1), F32)),
        grid=(bsz, heads, nc),
        in_specs=in_specs,
        out_specs=(pl.BlockSpec((chunk, dh), lambda b, h, c: (b * nc + c, h)),
                   c_spec, n_spec, m_spec),
        compiler_params=_params(("parallel", "parallel", "arbitrary")),
        name="mlstm",
    )(*args)
    return h, c_new, n_new.reshape(bsz, heads, dh), m_new.reshape(bsz, heads)


def _lambda(lam_ref, lam0):
    lp = lam_ref[...]
    return (jnp.exp(jnp.sum(lp[0:1] * lp[1:2], axis=1, keepdims=True))
            - jnp.exp(jnp.sum(lp[2:3] * lp[3:4], axis=1, keepdims=True)) + lam0)


def _alibi_slopes(heads):
    return np.asarray(2.0 ** (-8.0 * np.arange(1, heads + 1) / heads), np.float32)


def _dattn_prompt_kernel(slopes_ref, q_ref, k_ref, v_ref, lam_ref, hg_ref, o_ref, *, tq, nq, dk, lam0):
    hh = pl.program_id(1)
    qi = pl.program_id(2)
    lam = _lambda(lam_ref, lam0)
    slope = slopes_ref[hh]
    hg = hg_ref[pl.ds(hh, 1), :]

    def attend(q0, n_keys):
        q = q_ref[...]
        kb = k_ref[0:n_keys, :].astype(BF16)
        vb = v_ref[0:n_keys, :].astype(BF16)
        dist = (q0 + lax.broadcasted_iota(jnp.int32, (tq, n_keys), 0)
                - lax.broadcasted_iota(jnp.int32, (tq, n_keys), 1))
        keep = dist >= 0
        bias = slope * dist.astype(F32)
        probs = []
        for j in range(2):
            sc = lax.dot_general(q[:, j * dk:(j + 1) * dk], kb[:, j * dk:(j + 1) * dk], _NT,
                                 preferred_element_type=F32) * (dk ** -0.5)
            sc = jnp.where(keep, sc - bias, -jnp.inf)
            e = jnp.exp(sc - jnp.max(sc, axis=1, keepdims=True))
            probs.append(e / jnp.sum(e, axis=1, keepdims=True))
        a = (probs[0] - lam * probs[1]).astype(BF16)
        o = jnp.dot(a, vb, preferred_element_type=F32)
        o_ref[...] = (_rms(o, hg) * (1.0 - lam0)).astype(o_ref.dtype)

    for i in range(nq):
        pl.when(qi == i)(functools.partial(attend, i * tq, (i + 1) * tq))


def _dattn_prompt(q, k, v, lam_p, head_g, bsz, seq, lam0):
    heads, dv = head_g.shape
    dk = dv // 2
    tq = min(ATTN_Q_TILE, seq)
    nq = seq // tq
    kv_spec = pl.BlockSpec((seq, dv), lambda b, h, i: (b, h))
    q_spec = pl.BlockSpec((tq, dv), lambda b, h, i: (b * nq + i, h))
    return pl.pallas_call(
        functools.partial(_dattn_prompt_kernel, tq=tq, nq=nq, dk=dk, lam0=lam0),
        out_shape=jax.ShapeDtypeStruct(q.shape, BF16),
        grid=(bsz, heads, nq),
        in_specs=[pl.BlockSpec(memory_space=pltpu.SMEM),
                  q_spec, kv_spec, kv_spec,
                  pl.BlockSpec((4, dk), lambda b, h, i: (0, 0)),
                  pl.BlockSpec((heads, dv), lambda b, h, i: (0, 0))],
        out_specs=q_spec,
        compiler_params=_params(("parallel", "parallel", "arbitrary")),
        name="dattn_prompt",
    )(jnp.asarray(_alibi_slopes(heads)), q, k, v, lam_p, head_g)


def _dattn_sample_kernel(pt_ref, qt_ref, kc_ref, vc_ref, kn_ref, vn_ref, apast_ref, anew_ref, slope_ref,
                         lam_ref, gain_ref, o_ref, m_scr, l_scr, acc_scr, p_scr,
                         *, dk, page, n_pages, td, lam0):
    pg = pl.program_id(1)
    n_cols = qt_ref.shape[1]
    past_len = n_pages * page
    r2 = p_scr.shape[0] // 2

    def to_col(row):
        eye = (lax.broadcasted_iota(jnp.int32, (n_cols, n_cols), 0)
               == lax.broadcasted_iota(jnp.int32, (n_cols, n_cols), 1))
        return jnp.sum(jnp.where(eye, row, 0.0), axis=1, keepdims=True)

    @pl.when(pg == 0)
    def _init():
        m_scr[...] = jnp.full_like(m_scr, -jnp.inf)
        l_scr[...] = jnp.zeros_like(l_scr)
        acc_scr[...] = jnp.zeros_like(acc_scr)

    def process(k_ref, v_ref, a_ref, page_bias):
        kb = k_ref[...].astype(BF16)
        sc = jnp.dot(kb, qt_ref[...], preferred_element_type=F32) * (dk ** -0.5) + a_ref[...] + page_bias
        m_old = m_scr[...]
        m_new = jnp.maximum(m_old, jnp.max(sc, axis=0, keepdims=True))
        alpha = jnp.exp(m_old - m_new)
        p_scr[...] = jnp.exp(sc - m_new)
        l_scr[...] = alpha * l_scr[...] + jnp.sum(p_scr[...], axis=0, keepdims=True)
        m_scr[...] = m_new
        p2 = p_scr[pl.ds(0, r2, stride=2), :] + p_scr[pl.ds(1, r2, stride=2), :]
        pv = lax.dot_general(p2.astype(BF16), v_ref[...].astype(BF16), _TN, preferred_element_type=F32)
        acc_scr[...] = to_col(alpha) * acc_scr[...] + pv

    @pl.when(pg < n_pages)
    def _past():
        process(kc_ref, vc_ref, apast_ref, slope_ref[...] * (pg * page - past_len).astype(F32))

    @pl.when(pg == n_pages)
    def _new():
        process(kn_ref, vn_ref, anew_ref, 0.0)
        lam = _lambda(lam_ref, lam0)
        on = acc_scr[...] / to_col(l_scr[...])
        o = on - lam * pltpu.roll(on, n_cols - td, 0)
        o_ref[...] = _rms(o, gain_ref[...]) * (1.0 - lam0)


def _dattn_sample(q, cache_k, cache_v, page_table, k_new, v_new, lam_p, head_g, rows, td, lam0):
    heads, dv = head_g.shape
    dk = dv // 2
    bd, n_pages = page_table.shape
    n_pool, page = cache_k.shape[:2]
    hw = heads * dv
    n_real = heads * 2 * td
    n_cols = -(-n_real // V7X_LANES) * V7X_LANES
    r_k = page * heads * 2
    r_v = page * heads

    qt = q.reshape(bd, rows, heads * 2, dk)[:, :td].transpose(0, 3, 2, 1).reshape(bd, dk, n_real)
    qt = jnp.pad(qt, ((0, 0), (0, 0), (0, n_cols - n_real)))

    def page_rows(new, width):
        new = jnp.pad(new.reshape(bd, rows, hw), ((0, 0), (0, page - rows), (0, 0)))
        return new.reshape(-1, width)

    slopes = _alibi_slopes(heads)
    col = np.arange(n_cols)
    col_hj = np.where(col < n_real, col // td, -1)
    col_t = col % td
    col_slope = np.where(col < n_real, slopes[np.minimum(col // (2 * td), heads - 1)], 0.0).astype(np.float32)
    row = np.arange(r_k)
    row_tok, row_hj = row // (heads * 2), row % (heads * 2)
    rel = (col_t[None, :] - row_tok[:, None]).astype(np.float32)
    match = (row_hj[:, None] == col_hj[None, :]) | (col_hj[None, :] < 0)
    a_past = np.where(match, -col_slope[None, :] * rel, -np.inf).astype(np.float32)
    a_new = np.where(match & (rel >= 0), -col_slope[None, :] * rel, -np.inf).astype(np.float32)
    a_new[:, n_real:] = 0.0
    gain = jnp.pad(jnp.repeat(head_g, 2 * td, axis=0), ((0, n_cols - n_real), (0, 0)), constant_values=1.0)

    def page_map(b, p, pt):
        return (pt[b * n_pages + jnp.minimum(p, n_pages - 1)], 0)

    const = lambda b, p, pt: (0, 0)
    out = pl.pallas_call(
        functools.partial(_dattn_sample_kernel, dk=dk, page=page, n_pages=n_pages, td=td, lam0=lam0),
        out_shape=jax.ShapeDtypeStruct((bd, n_cols, dv), F32),
        grid_spec=pltpu.PrefetchScalarGridSpec(
            num_scalar_prefetch=1,
            grid=(bd, n_pages + 1),
            in_specs=[pl.BlockSpec((None, dk, n_cols), lambda b, p, pt: (b, 0, 0)),
                      pl.BlockSpec((r_k, dk), page_map),
                      pl.BlockSpec((r_v, dv), page_map),
                      pl.BlockSpec((r_k, dk), lambda b, p, pt: (b, 0)),
                      pl.BlockSpec((r_v, dv), lambda b, p, pt: (b, 0)),
                      pl.BlockSpec((r_k, n_cols), const),
                      pl.BlockSpec((r_k, n_cols), const),
                      pl.BlockSpec((1, n_cols), const),
                      pl.BlockSpec((4, dk), const),
                      pl.BlockSpec((n_cols, dv), const)],
            out_specs=pl.BlockSpec((None, n_cols, dv), lambda b, p, pt: (b, 0, 0)),
            scratch_shapes=[pltpu.VMEM((1, n_cols), F32), pltpu.VMEM((1, n_cols), F32),
                            pltpu.VMEM((n_cols, dv), F32), pltpu.VMEM((r_k, n_cols), F32)]),
        compiler_params=_params(("parallel", "arbitrary")),
        name="dattn_sample",
    )(page_table.reshape(-1), qt,
      cache_k.reshape(n_pool * r_k, dk), cache_v.reshape(n_pool * r_v, dv),
      page_rows(k_new, dk), page_rows(v_new, dv),
      jnp.asarray(a_past), jnp.asarray(a_new), jnp.asarray(col_slope.reshape(1, n_cols)), lam_p, gain)
    o = out[:, :n_real].reshape(bd, heads, 2, td, dv)[:, :, 0].transpose(0, 2, 1, 3).reshape(bd, td, hw)
    return jnp.pad(o, ((0, 0), (0, rows - td), (0, 0))).reshape(bd * rows, hw).astype(BF16)


def _lambda_init(layer):
    return 0.8 - 0.6 * math.exp(-0.3 * layer)


def _trunk(xp, xs, bp, tp, bd, td, state, cache_k, cache_v, page_table, p):
    depth = p["norm_g"].shape[0]
    n_a = p["a_w_in"].shape[0]
    d = xp.shape[1]
    heads_a = p["a_head_g"].shape[1]
    hw = p["b_w_q"].shape[2]
    ts = SAMPLE_ROWS
    states = ([], [], [], [], [], [])
    kv_p = kv_s = None

    def both(fn, a, b, *args):
        return fn(a, *args), fn(b, *args)

    def resid(x, xs_, y, ys_, g_post, g_next, coef):
        (x, xn), (xs_, xns) = _resid_norm(x, y, g_post, g_next, coef), _resid_norm(xs_, ys_, g_post, g_next, coef)
        return x, xs_, xn, xns

    xn, xns = both(_rms_cast, xp, xs, p["norm_g"][0, 0])
    for l in range(depth):
        g = p["norm_g"][l]
        h, hs = _gate_up(xn, xns, p["ffn_w_gate"], p["ffn_w_up"], (l, 0))
        y, ys = _matmul(h, hs, p["ffn_w_down"], (l, 0), 0, d, F32, "ffn_down")
        xp, xs, xn, xns = resid(xp, xs, y, ys, g[1], g[2], 0.5)
        if l < n_a:
            w_in = p["a_w_in"]
            proj, projs = _matmul(xn, xns, w_in, (l,), 0, 4 * d, BF16, "mlstm_in")
            w_gates = jnp.pad(w_in[l, :, 4 * d:], ((0, 0), (0, V7X_LANES - 2 * heads_a)))
            gates, gatess = _matmul(xn, xns, w_gates, (), 0, V7X_LANES, F32, "mlstm_gates")
            bias = jnp.pad(p["a_b_if"][l], (0, V7X_LANES - 2 * heads_a)).reshape(1, V7X_LANES)
            hg = p["a_head_g"][l]
            hm, c_p, n_p, m_p = _mlstm(proj, gates, bias, hg, None, bp, tp, tp, min(MLSTM_CHUNK, tp))
            st = (state[0][l], state[1][l], state[2][l])
            hms, c_s, n_s, m_s = _mlstm(projs, gatess, bias, hg, st, bd, ts, td, ts)
            for acc, val in zip(states, (c_p, n_p, m_p, c_s, n_s, m_s)):
                acc.append(val)
            y, ys = _matmul(hm, hms, p["a_w_out"], (l,), 0, d, F32, "mlstm_out")
        else:
            j = l - n_a
            q, qs = _matmul(xn, xns, p["b_w_q"], (j,), 0, hw, BF16, "dattn_q")
            lam0 = _lambda_init(l)
            o = _dattn_prompt(q, kv_p[0], kv_p[1], p["b_lambda"][j], p["b_head_g"][j], bp, tp, lam0)
            os_ = _dattn_sample(qs, cache_k, cache_v, page_table, kv_s[0], kv_s[1],
                                p["b_lambda"][j], p["b_head_g"][j], ts, td, lam0)
            y, ys = _matmul(o, os_, p["b_w_o"], (j,), 0, d, F32, "dattn_o")
        xp, xs, xn, xns = resid(xp, xs, y, ys, g[3], g[4], 1.0)
        h, hs = _gate_up(xn, xns, p["ffn_w_gate"], p["ffn_w_up"], (l, 1))
        y, ys = _matmul(h, hs, p["ffn_w_down"], (l, 1), 0, d, F32, "ffn_down")
        g_next = p["norm_g"][l + 1, 0] if l + 1 < depth else None
        xp, xs, xn, xns = resid(xp, xs, y, ys, g[5], g_next, 0.5)
        if l == n_a - 1:
            xkv, xkvs = both(_rms_cast, xp, xs, p["kv_norm_g"])
            k_p, k_s = _matmul(xkv, xkvs, p["w_kv"], (), 0, hw, F32, "kv_k")
            v_p, v_s = _matmul(xkv, xkvs, p["w_kv"], (), hw, p["w_kv"].shape[1] - hw, F32, "kv_v")
            kv_p, kv_s = (k_p, v_p), (k_s, v_s)
    return xp, xs, [jnp.stack(s) for s in states], kv_p, kv_s


def kernel(x_prompt, x_sample, state_C, state_n, state_m, cache_k, cache_v, page_table, norm_g,
           ffn_w_gate, ffn_w_up, ffn_w_down, a_w_in, a_b_if, a_head_g, a_w_out, kv_norm_g, w_kv,
           b_w_q, b_lambda, b_head_g, b_w_o):
    p = dict(norm_g=norm_g, ffn_w_gate=ffn_w_gate, ffn_w_up=ffn_w_up, ffn_w_down=ffn_w_down,
             a_w_in=a_w_in, a_b_if=a_b_if, a_head_g=a_head_g, a_w_out=a_w_out, kv_norm_g=kv_norm_g,
             w_kv=w_kv, b_w_q=b_w_q, b_lambda=b_lambda, b_head_g=b_head_g, b_w_o=b_w_o)
    bp, tp, d = x_prompt.shape
    bd, td, _ = x_sample.shape
    heads_b, dv = b_head_g.shape[1:]
    dk = dv // 2
    assert td <= SAMPLE_ROWS
    xs = jnp.pad(x_sample, ((0, 0), (0, SAMPLE_ROWS - td), (0, 0))).reshape(bd * SAMPLE_ROWS, d)
    y_p, y_s, (c_p, n_p, m_p, c_s, n_s, m_s), (k_p, v_p), (k_s, v_s) = _trunk(
        x_prompt.reshape(bp * tp, d), xs, bp, tp, bd, td, (state_C, state_n, state_m),
        cache_k, cache_v, page_table, p)

    def unpad(a, shape):
        return a.reshape(bd, SAMPLE_ROWS, -1)[:, :td].reshape(shape)

    return (y_p.reshape(bp, tp, d), unpad(y_s, (bd, td, d)), c_p, n_p, m_p,
            k_p.reshape(bp, tp, heads_b, 2, dk), v_p.reshape(bp, tp, heads_b, dv),
            c_s, n_s, m_s,
            unpad(k_s, (bd, td, heads_b, 2, dk)), unpad(v_s, (bd, td, heads_b, dv)))
```

```python
import functools
import math

import numpy as np
import jax
import jax.numpy as jnp
from jax import lax
from jax.experimental import pallas as pl
from jax.experimental.pallas import tpu as pltpu

EPS = 1e-6
LOG2E = math.log2(math.e)
BF16 = jnp.bfloat16
F32 = jnp.float32

V7X_LANES = 128
V7X_MXU_COLS = 256
V7X_VMEM_BYTES = 64 * 1024 * 1024
VMEM_LIMIT_BYTES = V7X_VMEM_BYTES - 6 * 1024 * 1024

SAMPLE_ROWS = 16
ROW_TILE = 1024
COL_TILE = V7X_MXU_COLS
NORM_ROWS = 256
MLSTM_CHUNK = 256
ATTN_Q_TILE = 256

_NT = (((1,), (1,)), ((), ()))
_TN = (((0,), (0,)), ((), ()))
_NN = (((1,), (0,)), ((), ()))


def _params(semantics):
    return pltpu.CompilerParams(dimension_semantics=semantics, vmem_limit_bytes=VMEM_LIMIT_BYTES)


def _rows_spec(arr, block, index_map, **kwargs):
    if arr.ndim == 3:
        return pl.BlockSpec((None,) + block, lambda *g: (0,) + index_map(*g), **kwargs)
    return pl.BlockSpec(block, index_map, **kwargs)


def _rms(x, g):
    return x * lax.rsqrt(jnp.mean(x * x, axis=-1, keepdims=True) + EPS) * g


def _sigmoid(x):
    return 1.0 / (1.0 + jnp.exp(-x))


def _rms_cast_kernel(x_ref, g_ref, o_ref):
    o_ref[...] = _rms(x_ref[...], g_ref[...]).astype(o_ref.dtype)


def _rms_cast(x, g):
    m, d = x.shape
    tr = min(NORM_ROWS, m)
    return pl.pallas_call(
        _rms_cast_kernel,
        out_shape=jax.ShapeDtypeStruct((m, d), BF16),
        grid=(m // tr,),
        in_specs=[pl.BlockSpec((tr, d), lambda i: (i, 0)),
                  pl.BlockSpec((1, d), lambda i: (0, 0))],
        out_specs=pl.BlockSpec((tr, d), lambda i: (i, 0)),
        compiler_params=_params(("parallel",)),
        name="rms_cast",
    )(x, g.reshape(1, d))


def _resid_norm_kernel(x_ref, y_ref, gp_ref, gn_ref, xo_ref, xn_ref, *, coef):
    x = x_ref[...] + coef * _rms(y_ref[...], gp_ref[...])
    xo_ref[...] = x
    xn_ref[...] = _rms(x, gn_ref[...]).astype(xn_ref.dtype)


def _resid_kernel(x_ref, y_ref, gp_ref, xo_ref, *, coef):
    xo_ref[...] = x_ref[...] + coef * _rms(y_ref[...], gp_ref[...])


def _resid_norm(x, y, g_post, g_next, coef):
    m, d = x.shape
    tr = min(NORM_ROWS, m)
    row = pl.BlockSpec((tr, d), lambda i: (i, 0))
    y_row = _rows_spec(y, (tr, d), lambda i: (i, 0))
    vec = pl.BlockSpec((1, d), lambda i: (0, 0))
    if g_next is None:
        return pl.pallas_call(
            functools.partial(_resid_kernel, coef=coef),
            out_shape=jax.ShapeDtypeStruct((m, d), F32),
            grid=(m // tr,),
            in_specs=[row, y_row, vec],
            out_specs=row,
            compiler_params=_params(("parallel",)),
            name="resid",
        )(x, y, g_post.reshape(1, d)), None
    return pl.pallas_call(
        functools.partial(_resid_norm_kernel, coef=coef),
        out_shape=(jax.ShapeDtypeStruct((m, d), F32), jax.ShapeDtypeStruct((m, d), BF16)),
        grid=(m // tr,),
        in_specs=[row, y_row, vec, vec],
        out_specs=(row, row),
        compiler_params=_params(("parallel",)),
        name="resid_norm",
    )(x, y, g_post.reshape(1, d), g_next.reshape(1, d))


def _sample_rows(first_row_tile, compute, os_ref):
    @pl.when(first_row_tile)
    def _():
        os_ref[...] = compute().astype(os_ref.dtype)

    @pl.when(jnp.logical_not(first_row_tile))
    def _():
        os_ref[...] = jnp.zeros_like(os_ref)


def _mm_kernel(x_ref, xs_ref, w_ref, *out_refs, w_dims):
    o_ref, os_ref = out_refs[:2]
    w = w_ref[...].astype(BF16)
    y = lax.dot_general(x_ref[...], w, w_dims, preferred_element_type=F32)
    o_ref[...] = y.astype(o_ref.dtype)
    if len(out_refs) == 3:
        out_refs[2][...] = y.astype(BF16)
    _sample_rows(pl.program_id(0) == 0,
                 lambda: lax.dot_general(xs_ref[...], w, w_dims, preferred_element_type=F32), os_ref)


def _col_tile(k, n):
    tn = min(COL_TILE, n)
    while n % (2 * tn) == 0 and k * 2 * tn * 4 <= (8 << 20):
        tn *= 2
    return tn


def _matmul(x, xs, w, lead, n0, n, out_dtype, name, w_transposed=False, bf16_copy=False):
    m, k = x.shape
    ms = xs.shape[-2]
    tm = min(ROW_TILE, m)
    tn = _col_tile(k, n)
    assert m % tm == 0 and n % tn == 0 and n0 % tn == 0
    jb0 = n0 // tn
    nlead = len(lead)
    if w_transposed:
        w_spec = pl.BlockSpec((None,) * nlead + (tn, k), lambda i, j: lead + (j + jb0, 0))
    else:
        w_spec = pl.BlockSpec((None,) * nlead + (k, tn), lambda i, j: lead + (0, j + jb0))
    x_mode = dict(pipeline_mode=pl.Buffered(1)) if tm * k * 2 > (8 << 20) else {}
    tile = pl.BlockSpec((tm, tn), lambda i, j: (i, j))
    out_shape = [jax.ShapeDtypeStruct((m, n), out_dtype), jax.ShapeDtypeStruct((m // tm, ms, n), out_dtype)]
    out_specs = [tile, pl.BlockSpec((None, ms, tn), lambda i, j: (i, 0, j))]
    if bf16_copy:
        out_shape.append(jax.ShapeDtypeStruct((m, n), BF16))
        out_specs.append(tile)
    outs = pl.pallas_call(
        functools.partial(_mm_kernel, w_dims=_NT if w_transposed else _NN),
        out_shape=tuple(out_shape),
        grid=(m // tm, n // tn),
        in_specs=[pl.BlockSpec((tm, k), lambda i, j: (i, 0), **x_mode),
                  _rows_spec(xs, (ms, k), lambda i, j: (0, 0), **x_mode),
                  w_spec],
        out_specs=tuple(out_specs),
        compiler_params=_params(("arbitrary", "arbitrary")),
        name=name,
    )(x, xs, w)
    return outs


def _gate_up_kernel(x_ref, xs_ref, wg_ref, wu_ref, o_ref, os_ref):
    wg = wg_ref[...].astype(BF16)
    wu = wu_ref[...].astype(BF16)

    def swiglu(x):
        g = jnp.dot(x, wg, preferred_element_type=F32)
        u = jnp.dot(x, wu, preferred_element_type=F32)
        return g * _sigmoid(g) * u

    o_ref[...] = swiglu(x_ref[...]).astype(o_ref.dtype)
    _sample_rows(pl.program_id(0) == 0, lambda: swiglu(xs_ref[...]), os_ref)


def _gate_up(x, xs, w_gate, w_up, lead):
    m, k = x.shape
    ms = xs.shape[-2]
    f = w_gate.shape[-1]
    tm = min(ROW_TILE, m)
    tn = COL_TILE
    assert m % tm == 0 and f % tn == 0
    w_spec = pl.BlockSpec((None, None, k, tn), lambda i, j: lead + (0, j))
    return pl.pallas_call(
        _gate_up_kernel,
        out_shape=(jax.ShapeDtypeStruct((m, f), BF16), jax.ShapeDtypeStruct((m // tm, ms, f), BF16)),
        grid=(m // tm, f // tn),
        in_specs=[pl.BlockSpec((tm, k), lambda i, j: (i, 0)),
                  _rows_spec(xs, (ms, k), lambda i, j: (0, 0)), w_spec, w_spec],
        out_specs=(pl.BlockSpec((tm, tn), lambda i, j: (i, j)),
                   pl.BlockSpec((None, ms, tn), lambda i, j: (i, 0, j))),
        compiler_params=_params(("arbitrary", "arbitrary")),
        name="ffn_gate_up",
    )(x, xs, w_gate, w_up)


def _mlstm_kernel(*refs, chunk, valid, heads, has_state, scale):
    if has_state:
        (q_ref, k_ref, v_ref, o_ref, gc_ref, bias_ref, hg_ref, c0_ref, n0_ref, m0_ref,
         h_ref, c_ref, n_ref, m_ref) = refs
    else:
        (q_ref, k_ref, v_ref, o_ref, gc_ref, bias_ref, hg_ref,
         h_ref, c_ref, n_ref, m_ref) = refs
    hh = pl.program_id(1)
    cc = pl.program_id(2)
    L = chunk

    @pl.when(cc == 0)
    def _init():
        if has_state:
            c_ref[...] = c0_ref[...]
            n_ref[...] = n0_ref[...]
            m_ref[...] = m0_ref[...]
        else:
            c_ref[...] = jnp.zeros_like(c_ref)
            n_ref[...] = jnp.zeros_like(n_ref)
            m_ref[...] = jnp.zeros_like(m_ref)

    gc = gc_ref[...] + bias_ref[...]
    lane = lax.broadcasted_iota(jnp.int32, gc.shape, 1)
    li_col = jnp.sum(jnp.where(lane == hh, gc, 0.0), axis=1, keepdims=True)
    f_col = jnp.sum(jnp.where(lane == hh + heads, gc, 0.0), axis=1, keepdims=True)
    lf_col = jnp.minimum(f_col, 0.0) - jnp.log(1.0 + jnp.exp(-jnp.abs(f_col)))
    if valid < L:
        rid = lax.broadcasted_iota(jnp.int32, (L, 1), 0)
        li_col = jnp.where(rid < valid, li_col, -jnp.inf)
        lf_col = jnp.where(rid < valid, lf_col, 0.0)

    r = lax.broadcasted_iota(jnp.int32, (L, L), 0)
    s = lax.broadcasted_iota(jnp.int32, (L, L), 1)
    causal = s <= r
    li_row = jnp.sum(jnp.where(r == s, li_col, 0.0), axis=0, keepdims=True)
    lf_row = jnp.sum(jnp.where(r == s, lf_col, 0.0), axis=0, keepdims=True)
    b_col = jnp.sum(jnp.where(causal, lf_row, 0.0), axis=1, keepdims=True)
    b_row = jnp.sum(jnp.where(r <= s, lf_col, 0.0), axis=0, keepdims=True)

    m_prev = m_ref[...]
    m_inter = b_col + m_prev
    log_d = jnp.where(causal, b_col - b_row + li_row, -jnp.inf)
    m_t = jnp.maximum(m_inter, jnp.max(log_d, axis=1, keepdims=True))
    decay_mat = jnp.exp(log_d - m_t)
    inter = jnp.exp(m_inter - m_t)

    q = q_ref[...]
    k = k_ref[...]
    v = v_ref[...]
    c_state = c_ref[...]
    n_state = n_ref[...]
    smat = lax.dot_general(q, k, _NT, preferred_element_type=F32) * scale * decay_mat
    q_c = lax.dot_general(q, c_state.astype(BF16), _NT, preferred_element_type=F32)
    num = jnp.dot(smat.astype(BF16), v, preferred_element_type=F32) + inter * q_c
    q_n = jnp.sum(q.astype(F32) * n_state, axis=1, keepdims=True)
    den = jnp.sum(smat, axis=1, keepdims=True) + inter * q_n
    hval = num / jnp.maximum(jnp.abs(den), jnp.exp(-m_t))

    hg = hg_ref[pl.ds(hh, 1), :]
    h_ref[...] = (_rms(hval, hg) * _sigmoid(o_ref[...].astype(F32))).astype(h_ref.dtype)

    b_last = jnp.sum(lf_col, axis=0, keepdims=True)
    log_w = b_last - b_col + li_col
    m_new = jnp.maximum(b_last + m_prev, jnp.max(log_w, axis=0, keepdims=True))
    w_col = jnp.exp(log_w - m_new)
    decay = jnp.exp(b_last + m_prev - m_new)
    k_scaled = k.astype(F32) * scale
    v_w = (v.astype(F32) * w_col).astype(BF16)
    c_ref[...] = decay * c_state + lax.dot_general(v_w, k_scaled.astype(BF16), _TN,
                                                   preferred_element_type=F32)
    n_ref[...] = decay * n_state + jnp.sum(k_scaled * w_col, axis=0, keepdims=True)
    m_ref[...] = m_new


def _mlstm(proj, gates, bias, head_g, state, layer, bsz, seq, valid, chunk):
    heads, dh = head_g.shape
    d = heads * dh
    nc = seq // chunk
    assert seq % chunk == 0
    has_state = state is not None

    def col(part):
        return _rows_spec(proj, (chunk, dh), lambda b, h, c: (b * nc + c, part * heads + h))

    c_spec = pl.BlockSpec((None, None, dh, dh), lambda b, h, c: (b, h, 0, 0))
    n_spec = pl.BlockSpec((None, None, 1, dh), lambda b, h, c: (b, h, 0, 0))
    m_spec = pl.BlockSpec((None, None, 1, 1), lambda b, h, c: (b, h, 0, 0))
    in_specs = [col(0), col(1), col(2), col(3),
                _rows_spec(gates, (chunk, V7X_LANES), lambda b, h, c: (b * nc + c, 0)),
                pl.BlockSpec((1, V7X_LANES), lambda b, h, c: (0, 0)),
                pl.BlockSpec((heads, dh), lambda b, h, c: (0, 0))]
    args = [proj, proj, proj, proj, gates, bias, head_g]
    if has_state:
        c0, n0, m0 = state
        n_layers = c0.shape[0]
        in_specs += [pl.BlockSpec((None, None, None, dh, dh), lambda b, h, c: (layer, b, h, 0, 0)),
                     pl.BlockSpec((None, None, None, 1, dh), lambda b, h, c: (layer, b, h, 0, 0)),
                     pl.BlockSpec((None, None, None, 1, 1), lambda b, h, c: (layer, b, h, 0, 0))]
        args += [c0, n0.reshape(n_layers, bsz, heads, 1, dh), m0.reshape(n_layers, bsz, heads, 1, 1)]
    h, c_new, n_new, m_new = pl.pallas_call(
        functools.partial(_mlstm_kernel, chunk=chunk, valid=valid, heads=heads,
                          has_state=has_state, scale=dh ** -0.5),
        out_shape=(jax.ShapeDtypeStruct((bsz * seq, d), BF16),
                   jax.ShapeDtypeStruct((bsz, heads, dh, dh), F32),
                   jax.ShapeDtypeStruct((bsz, heads, 1, dh), F32),
                   jax.ShapeDtypeStruct((bsz, heads, 1, 1), F32)),
        grid=(bsz, heads, nc),
        in_specs=in_specs,
        out_specs=(pl.BlockSpec((chunk, dh), lambda b, h, c: (b * nc + c, h)),
                   c_spec, n_spec, m_spec),
        compiler_params=_params(("parallel", "parallel", "arbitrary")),
        name="mlstm",
    )(*args)
    return h, c_new, n_new.reshape(bsz, heads, dh), m_new.reshape(bsz, heads)


def _lambda(lam_ref, lam0):
    lp = lam_ref[...]
    return (jnp.exp(jnp.sum(lp[0:1] * lp[1:2], axis=1, keepdims=True))
            - jnp.exp(jnp.sum(lp[2:3] * lp[3:4], axis=1, keepdims=True)) + lam0)


def _alibi_slopes(heads):
    return np.asarray(2.0 ** (-8.0 * np.arange(1, heads + 1) / heads), np.float32)


def _dattn_prompt_kernel(slopes_ref, q_ref, k_ref, v_ref, lam_ref, hg_ref, o_ref, *, tq, nq, dk, lam0):
    hh = pl.program_id(1)
    qi = pl.program_id(2)
    lam = _lambda(lam_ref, lam0)
    slope = slopes_ref[hh]
    hg = hg_ref[pl.ds(hh, 1), :]

    def attend(q0, n_keys):
        q = q_ref[...]
        kb = k_ref[0:n_keys, :]
        vb = v_ref[0:n_keys, :]
        dist = (q0 + lax.broadcasted_iota(jnp.int32, (tq, n_keys), 0)
                - lax.broadcasted_iota(jnp.int32, (tq, n_keys), 1))
        keep = dist >= 0
        bias = (slope * LOG2E) * dist.astype(F32)
        outs = []
        for j in range(2):
            sc = lax.dot_general(q[:, j * dk:(j + 1) * dk], kb[:, j * dk:(j + 1) * dk], _NT,
                                 preferred_element_type=F32) * (dk ** -0.5 * LOG2E)
            sc = jnp.where(keep, sc - bias, -jnp.inf)
            e = jnp.exp2(sc - jnp.max(sc, axis=1, keepdims=True))
            outs.append(jnp.dot(e.astype(BF16), vb, preferred_element_type=F32)
                        / jnp.sum(e, axis=1, keepdims=True))
        o = outs[0] - lam * outs[1]
        o_ref[...] = (_rms(o, hg) * (1.0 - lam0)).astype(o_ref.dtype)

    for i in range(nq):
        pl.when(qi == i)(functools.partial(attend, i * tq, (i + 1) * tq))


def _dattn_prompt(q, k, v, lam_p, head_g, bsz, seq, lam0):
    heads, dv = head_g.shape
    dk = dv // 2
    tq = min(ATTN_Q_TILE, seq)
    nq = seq // tq
    kv_spec = pl.BlockSpec((seq, dv), lambda b, h, i: (b, h))
    q_spec = pl.BlockSpec((tq, dv), lambda b, h, i: (b * nq + i, h))
    return pl.pallas_call(
        functools.partial(_dattn_prompt_kernel, tq=tq, nq=nq, dk=dk, lam0=lam0),
        out_shape=jax.ShapeDtypeStruct(q.shape, BF16),
        grid=(bsz, heads, nq),
        in_specs=[pl.BlockSpec(memory_space=pltpu.SMEM),
                  q_spec, kv_spec, kv_spec,
                  pl.BlockSpec((4, dk), lambda b, h, i: (0, 0)),
                  pl.BlockSpec((heads, dv), lambda b, h, i: (0, 0))],
        out_specs=q_spec,
        compiler_params=_params(("parallel", "parallel", "arbitrary")),
        name="dattn_prompt",
    )(jnp.asarray(_alibi_slopes(heads)), q, k, v, lam_p, head_g)


def _dattn_sample_kernel(pt_ref, qt_ref, kc_ref, vc_ref, kn_ref, vn_ref, apast_ref, anew_ref, slope_ref,
                         lam_ref, gain_ref, o_ref, m_scr, l_scr, acc_scr, *, dk, page, n_pages, td, lam0):
    pg = pl.program_id(1)
    n_cols = qt_ref.shape[1]
    n_rows = vc_ref.shape[0]
    past_len = n_pages * page

    def to_col(row):
        eye = (lax.broadcasted_iota(jnp.int32, (n_cols, n_cols), 0)
               == lax.broadcasted_iota(jnp.int32, (n_cols, n_cols), 1))
        return jnp.sum(jnp.where(eye, row, 0.0), axis=1, keepdims=True)

    @pl.when(pg == 0)
    def _init():
        m_scr[...] = jnp.full_like(m_scr, -jnp.inf)
        l_scr[...] = jnp.zeros_like(l_scr)
        acc_scr[...] = jnp.zeros_like(acc_scr)

    def process(k_ref, v_ref, a_ref, page_bias):
        kb = jnp.concatenate([k_ref[pl.ds(0, n_rows, stride=2), :].astype(BF16),
                              k_ref[pl.ds(1, n_rows, stride=2), :].astype(BF16)], axis=1)
        sc = jnp.dot(kb, qt_ref[...], preferred_element_type=F32) * (dk ** -0.5) + a_ref[...] + page_bias
        m_old = m_scr[...]
        m_new = jnp.maximum(m_old, jnp.max(sc, axis=0, keepdims=True))
        alpha = jnp.exp(m_old - m_new)
        p = jnp.exp(sc - m_new)
        l_scr[...] = alpha * l_scr[...] + jnp.sum(p, axis=0, keepdims=True)
        m_scr[...] = m_new
        pv = lax.dot_general(p.astype(BF16), v_ref[...].astype(BF16), _TN, preferred_element_type=F32)
        acc_scr[...] = to_col(alpha) * acc_scr[...] + pv

    @pl.when(pg < n_pages)
    def _past():
        process(kc_ref, vc_ref, apast_ref, slope_ref[...] * (pg * page - past_len).astype(F32))

    @pl.when(pg == n_pages)
    def _new():
        process(kn_ref, vn_ref, anew_ref, 0.0)
        lam = _lambda(lam_ref, lam0)
        on = acc_scr[...] / to_col(l_scr[...])
        o = on - lam * pltpu.roll(on, n_cols - td, 0)
        o_ref[...] = _rms(o, gain_ref[...]) * (1.0 - lam0)


def _dattn_sample(q, cache_k, cache_v, page_table, k_new, v_new, lam_p, head_g, rows, td, lam0):
    heads, dv = head_g.shape
    dk = dv // 2
    bd, n_pages = page_table.shape
    n_pool, page = cache_k.shape[:2]
    hw = heads * dv
    q, k_new, v_new = (a[0] if a.ndim == 3 else a for a in (q, k_new, v_new))
    n_real = heads * 2 * td
    n_cols = -(-n_real // V7X_LANES) * V7X_LANES
    n_rows = page * heads

    col = np.arange(n_cols)
    real = col < n_real
    col_h = np.where(real, col // (2 * td), -1)
    col_j = (col // td) % 2
    col_t = col % td
    qt = q.reshape(bd, rows, heads * 2, dk)[:, :td].transpose(0, 3, 2, 1).reshape(bd, dk, n_real)
    qt = jnp.pad(qt, ((0, 0), (0, 0), (0, n_cols - n_real)))
    qt = jnp.concatenate([jnp.where(col_j == j, qt, jnp.zeros_like(qt)) for j in range(2)], axis=1)

    def page_rows(new, width):
        new = jnp.pad(new.reshape(bd, rows, hw), ((0, 0), (0, page - rows), (0, 0)))
        return new.reshape(-1, width)

    col_slope = np.where(real, _alibi_slopes(heads)[np.maximum(col_h, 0)], 0.0).astype(np.float32)
    row = np.arange(n_rows)
    row_tok, row_h = row // heads, row % heads
    rel = (col_t[None, :] - row_tok[:, None]).astype(np.float32)
    match = (row_h[:, None] == col_h[None, :]) | ~real[None, :]
    a_past = np.where(match, -col_slope[None, :] * rel, -np.inf).astype(np.float32)
    a_new = np.where(match & ((rel >= 0) | ~real[None, :]), -col_slope[None, :] * rel, -np.inf).astype(np.float32)
    gain = jnp.pad(jnp.repeat(head_g, 2 * td, axis=0), ((0, n_cols - n_real), (0, 0)), constant_values=1.0)

    def page_map(b, p, pt):
        return (pt[b * n_pages + jnp.minimum(p, n_pages - 1)], 0)

    const = lambda b, p, pt: (0, 0)
    out = pl.pallas_call(
        functools.partial(_dattn_sample_kernel, dk=dk, page=page, n_pages=n_pages, td=td, lam0=lam0),
        out_shape=jax.ShapeDtypeStruct((bd, n_cols, dv), F32),
        grid_spec=pltpu.PrefetchScalarGridSpec(
            num_scalar_prefetch=1,
            grid=(bd, n_pages + 1),
            in_specs=[pl.BlockSpec((None, 2 * dk, n_cols), lambda b, p, pt: (b, 0, 0)),
                      pl.BlockSpec((2 * n_rows, dk), page_map),
                      pl.BlockSpec((n_rows, dv), page_map),
                      pl.BlockSpec((2 * n_rows, dk), lambda b, p, pt: (b, 0)),
                      pl.BlockSpec((n_rows, dv), lambda b, p, pt: (b, 0)),
                      pl.BlockSpec((n_rows, n_cols), const),
                      pl.BlockSpec((n_rows, n_cols), const),
                      pl.BlockSpec((1, n_cols), const),
                      pl.BlockSpec((4, dk), const),
                      pl.BlockSpec((n_cols, dv), const)],
            out_specs=pl.BlockSpec((None, n_cols, dv), lambda b, p, pt: (b, 0, 0)),
            scratch_shapes=[pltpu.VMEM((1, n_cols), F32), pltpu.VMEM((1, n_cols), F32),
                            pltpu.VMEM((n_cols, dv), F32)]),
        compiler_params=_params(("parallel", "arbitrary")),
        name="dattn_sample",
    )(page_table.reshape(-1), qt,
      cache_k.reshape(n_pool * 2 * n_rows, dk), cache_v.reshape(n_pool * n_rows, dv),
      page_rows(k_new, dk), page_rows(v_new, dv),
      jnp.asarray(a_past), jnp.asarray(a_new), jnp.asarray(col_slope.reshape(1, n_cols)), lam_p, gain)
    o = out[:, :n_real].reshape(bd, heads, 2, td, dv)[:, :, 0].transpose(0, 2, 1, 3).reshape(bd, td, hw)
    return jnp.pad(o, ((0, 0), (0, rows - td), (0, 0))).reshape(bd * rows, hw).astype(BF16)


def _lambda_init(layer):
    return 0.8 - 0.6 * math.exp(-0.3 * layer)


def _trunk(xp, xs, bp, tp, bd, td, state, cache_k, cache_v, page_table, p):
    depth = p["norm_g"].shape[0]
    n_a = p["a_w_in"].shape[0]
    d = xp.shape[1]
    heads_a = p["a_head_g"].shape[1]
    hw = p["b_w_q"].shape[2]
    ts = SAMPLE_ROWS
    states = ([], [], [], [], [], [])
    kv_p = kv_s = kv_bf = None
    w_in_t = p["a_w_in"].transpose(0, 2, 1)

    def both(fn, a, b, *args):
        return fn(a, *args), fn(b, *args)

    def resid(x, xs_, y, ys_, g_post, g_next, coef):
        (x, xn), (xs_, xns) = _resid_norm(x, y, g_post, g_next, coef), _resid_norm(xs_, ys_, g_post, g_next, coef)
        return x, xs_, xn, xns

    xn, xns = both(_rms_cast, xp, xs, p["norm_g"][0, 0])
    for l in range(depth):
        g = p["norm_g"][l]
        h, hs = _gate_up(xn, xns, p["ffn_w_gate"], p["ffn_w_up"], (l, 0))
        y, ys = _matmul(h, hs, p["ffn_w_down"], (l, 0), 0, d, F32, "ffn_down")
        xp, xs, xn, xns = resid(xp, xs, y, ys, g[1], g[2], 0.5)
        if l < n_a:
            proj, projs = _matmul(xn, xns, w_in_t, (l,), 0, 4 * d, BF16, "mlstm_in", w_transposed=True)
            w_gates = jnp.pad(w_in_t[l, 4 * d:, :], ((0, V7X_LANES - 2 * heads_a), (0, 0)))
            gates, gatess = _matmul(xn, xns, w_gates, (), 0, V7X_LANES, F32, "mlstm_gates", w_transposed=True)
            bias = jnp.pad(p["a_b_if"][l], (0, V7X_LANES - 2 * heads_a)).reshape(1, V7X_LANES)
            hg = p["a_head_g"][l]
            hm, c_p, n_p, m_p = _mlstm(proj, gates, bias, hg, None, l, bp, tp, tp, min(MLSTM_CHUNK, tp))
            hms, c_s, n_s, m_s = _mlstm(projs, gatess, bias, hg, state, l, bd, ts, td, ts)
            for acc, val in zip(states, (c_p, n_p, m_p, c_s, n_s, m_s)):
                acc.append(val)
            y, ys = _matmul(hm, hms, p["a_w_out"], (l,), 0, d, F32, "mlstm_out")
        else:
            j = l - n_a
            q, qs = _matmul(xn, xns, p["b_w_q"], (j,), 0, hw, BF16, "dattn_q")
            lam0 = _lambda_init(l)
            o = _dattn_prompt(q, kv_bf[0], kv_bf[1], p["b_lambda"][j], p["b_head_g"][j], bp, tp, lam0)
            os_ = _dattn_sample(qs, cache_k, cache_v, page_table, kv_s[0], kv_s[1],
                                p["b_lambda"][j], p["b_head_g"][j], ts, td, lam0)
            y, ys = _matmul(o, os_, p["b_w_o"], (j,), 0, d, F32, "dattn_o")
        xp, xs, xn, xns = resid(xp, xs, y, ys, g[3], g[4], 1.0)
        h, hs = _gate_up(xn, xns, p["ffn_w_gate"], p["ffn_w_up"], (l, 1))
        y, ys = _matmul(h, hs, p["ffn_w_down"], (l, 1), 0, d, F32, "ffn_down")
        g_next = p["norm_g"][l + 1, 0] if l + 1 < depth else None
        xp, xs, xn, xns = resid(xp, xs, y, ys, g[5], g_next, 0.5)
        if l == n_a - 1:
            xkv, xkvs = both(_rms_cast, xp, xs, p["kv_norm_g"])
            k_p, k_s, k_bf = _matmul(xkv, xkvs, p["w_kv"], (), 0, hw, F32, "kv_k", bf16_copy=True)
            v_p, v_s, v_bf = _matmul(xkv, xkvs, p["w_kv"], (), hw, p["w_kv"].shape[1] - hw, F32, "kv_v",
                                     bf16_copy=True)
            kv_p, kv_s, kv_bf = (k_p, v_p), (k_s, v_s), (k_bf, v_bf)
    return xp, xs, [jnp.stack(s) for s in states], kv_p, kv_s


def kernel(x_prompt, x_sample, state_C, state_n, state_m, cache_k, cache_v, page_table, norm_g,
           ffn_w_gate, ffn_w_up, ffn_w_down, a_w_in, a_b_if, a_head_g, a_w_out, kv_norm_g, w_kv,
           b_w_q, b_lambda, b_head_g, b_w_o):
    p = dict(norm_g=norm_g, ffn_w_gate=ffn_w_gate, ffn_w_up=ffn_w_up, ffn_w_down=ffn_w_down,
             a_w_in=a_w_in, a_b_if=a_b_if, a_head_g=a_head_g, a_w_out=a_w_out, kv_norm_g=kv_norm_g,
             w_kv=w_kv, b_w_q=b_w_q, b_lambda=b_lambda, b_head_g=b_head_g, b_w_o=b_w_o)
    bp, tp, d = x_prompt.shape
    bd, td, _ = x_sample.shape
    heads_b, dv = b_head_g.shape[1:]
    dk = dv // 2
    assert td <= SAMPLE_ROWS
    xs = jnp.pad(x_sample, ((0, 0), (0, SAMPLE_ROWS - td), (0, 0))).reshape(bd * SAMPLE_ROWS, d)
    y_p, y_s, (c_p, n_p, m_p, c_s, n_s, m_s), (k_p, v_p), (k_s, v_s) = _trunk(
        x_prompt.reshape(bp * tp, d), xs, bp, tp, bd, td, (state_C, state_n, state_m),
        cache_k, cache_v, page_table, p)

    def unpad(a, shape):
        a = a[0] if a.ndim == 3 else a
        return a.reshape(bd, SAMPLE_ROWS, -1)[:, :td].reshape(shape)

    return (y_p.reshape(bp, tp, d), unpad(y_s, (bd, td, d)), c_p, n_p, m_p,
            k_p.reshape(bp, tp, heads_b, 2, dk), v_p.reshape(bp, tp, heads_b, dv),
            c_s, n_s, m_s,
            unpad(k_s, (bd, td, heads_b, 2, dk)), unpad(v_s, (bd, td, heads_b, dv)))
```

```python
import functools
import math

import numpy as np
import jax
import jax.numpy as jnp
from jax import lax
from jax.experimental import pallas as pl
from jax.experimental.pallas import tpu as pltpu

EPS = 1e-6
LOG2E = math.log2(math.e)
BF16 = jnp.bfloat16
F32 = jnp.float32

V7X_LANES = 128
V7X_MXU_COLS = 256
V7X_VMEM_BYTES = 64 * 1024 * 1024
VMEM_LIMIT_BYTES = V7X_VMEM_BYTES - 6 * 1024 * 1024

SAMPLE_ROWS = 16
ROW_TILE = 1024
COL_TILE = V7X_MXU_COLS
NORM_ROWS = 256
MLSTM_CHUNK = 256
ATTN_Q_TILE = 256
SAMPLE_PAGES_PER_STEP = 4

_NT = (((1,), (1,)), ((), ()))
_TN = (((0,), (0,)), ((), ()))
_NN = (((1,), (0,)), ((), ()))


def _params(semantics):
    return pltpu.CompilerParams(dimension_semantics=semantics, vmem_limit_bytes=VMEM_LIMIT_BYTES)


def _rows_spec(arr, block, index_map, **kwargs):
    if arr.ndim == 3:
        return pl.BlockSpec((None,) + block, lambda *g: (0,) + index_map(*g), **kwargs)
    return pl.BlockSpec(block, index_map, **kwargs)


def _rms(x, g):
    return x * lax.rsqrt(jnp.mean(x * x, axis=-1, keepdims=True) + EPS) * g


def _sigmoid(x):
    return 1.0 / (1.0 + jnp.exp(-x))


def _rms_cast_kernel(x_ref, g_ref, o_ref):
    o_ref[...] = _rms(x_ref[...], g_ref[...]).astype(o_ref.dtype)


def _rms_cast(x, g):
    m, d = x.shape
    tr = min(NORM_ROWS, m)
    return pl.pallas_call(
        _rms_cast_kernel,
        out_shape=jax.ShapeDtypeStruct((m, d), BF16),
        grid=(m // tr,),
        in_specs=[pl.BlockSpec((tr, d), lambda i: (i, 0)),
                  pl.BlockSpec((1, d), lambda i: (0, 0))],
        out_specs=pl.BlockSpec((tr, d), lambda i: (i, 0)),
        compiler_params=_params(("parallel",)),
        name="rms_cast",
    )(x, g.reshape(1, d))


def _resid_norm_kernel(x_ref, y_ref, gp_ref, gn_ref, xo_ref, xn_ref, *, coef):
    x = x_ref[...] + coef * _rms(y_ref[...].astype(F32), gp_ref[...])
    xo_ref[...] = x
    xn_ref[...] = _rms(x, gn_ref[...]).astype(xn_ref.dtype)


def _resid_kernel(x_ref, y_ref, gp_ref, xo_ref, *, coef):
    xo_ref[...] = x_ref[...] + coef * _rms(y_ref[...].astype(F32), gp_ref[...])


def _resid_norm(x, y, g_post, g_next, coef):
    m, d = x.shape
    tr = min(NORM_ROWS, m)
    row = pl.BlockSpec((tr, d), lambda i: (i, 0))
    y_row = _rows_spec(y, (tr, d), lambda i: (i, 0))
    vec = pl.BlockSpec((1, d), lambda i: (0, 0))
    if g_next is None:
        return pl.pallas_call(
            functools.partial(_resid_kernel, coef=coef),
            out_shape=jax.ShapeDtypeStruct((m, d), F32),
            grid=(m // tr,),
            in_specs=[row, y_row, vec],
            out_specs=row,
            compiler_params=_params(("parallel",)),
            name="resid",
        )(x, y, g_post.reshape(1, d)), None
    return pl.pallas_call(
        functools.partial(_resid_norm_kernel, coef=coef),
        out_shape=(jax.ShapeDtypeStruct((m, d), F32), jax.ShapeDtypeStruct((m, d), BF16)),
        grid=(m // tr,),
        in_specs=[row, y_row, vec, vec],
        out_specs=(row, row),
        compiler_params=_params(("parallel",)),
        name="resid_norm",
    )(x, y, g_post.reshape(1, d), g_next.reshape(1, d))


def _sample_rows(first_row_tile, compute, os_ref):
    @pl.when(first_row_tile)
    def _():
        os_ref[...] = compute().astype(os_ref.dtype)

    @pl.when(jnp.logical_not(first_row_tile))
    def _():
        os_ref[...] = jnp.zeros_like(os_ref)


def _mm_kernel(x_ref, xs_ref, w_ref, *out_refs, w_dims):
    o_ref, os_ref = out_refs[:2]
    w = w_ref[...].astype(BF16)
    y = lax.dot_general(x_ref[...], w, w_dims, preferred_element_type=F32)
    o_ref[...] = y.astype(o_ref.dtype)
    if len(out_refs) == 3:
        out_refs[2][...] = y.astype(BF16)
    _sample_rows(pl.program_id(0) == 0,
                 lambda: lax.dot_general(xs_ref[...], w, w_dims, preferred_element_type=F32), os_ref)


def _col_tile(k, n):
    tn = min(COL_TILE, n)
    while n % (2 * tn) == 0 and k * 2 * tn * 4 <= (8 << 20):
        tn *= 2
    return tn


def _matmul(x, xs, w, lead, n0, n, out_dtype, name, w_transposed=False, bf16_copy=False):
    m, k = x.shape
    ms = xs.shape[-2]
    tm = min(ROW_TILE, m)
    tn = _col_tile(k, n)
    assert m % tm == 0 and n % tn == 0 and n0 % tn == 0
    jb0 = n0 // tn
    nlead = len(lead)
    if w_transposed:
        w_spec = pl.BlockSpec((None,) * nlead + (tn, k), lambda i, j: lead + (j + jb0, 0))
    else:
        w_spec = pl.BlockSpec((None,) * nlead + (k, tn), lambda i, j: lead + (0, j + jb0))
    x_mode = dict(pipeline_mode=pl.Buffered(1)) if tm * k * 2 > (8 << 20) else {}
    tile = pl.BlockSpec((tm, tn), lambda i, j: (i, j))
    out_shape = [jax.ShapeDtypeStruct((m, n), out_dtype), jax.ShapeDtypeStruct((m // tm, ms, n), out_dtype)]
    out_specs = [tile, pl.BlockSpec((None, ms, tn), lambda i, j: (i, 0, j))]
    if bf16_copy:
        out_shape.append(jax.ShapeDtypeStruct((m, n), BF16))
        out_specs.append(tile)
    outs = pl.pallas_call(
        functools.partial(_mm_kernel, w_dims=_NT if w_transposed else _NN),
        out_shape=tuple(out_shape),
        grid=(m // tm, n // tn),
        in_specs=[pl.BlockSpec((tm, k), lambda i, j: (i, 0), **x_mode),
                  _rows_spec(xs, (ms, k), lambda i, j: (0, 0), **x_mode),
                  w_spec],
        out_specs=tuple(out_specs),
        compiler_params=_params(("arbitrary", "arbitrary")),
        name=name,
    )(x, xs, w)
    return outs


def _gate_up_kernel(x_ref, xs_ref, wg_ref, wu_ref, o_ref, os_ref):
    wg = wg_ref[...].astype(BF16)
    wu = wu_ref[...].astype(BF16)

    def swiglu(x):
        g = jnp.dot(x, wg, preferred_element_type=F32)
        u = jnp.dot(x, wu, preferred_element_type=F32)
        return g * _sigmoid(g) * u

    o_ref[...] = swiglu(x_ref[...]).astype(o_ref.dtype)
    _sample_rows(pl.program_id(0) == 0, lambda: swiglu(xs_ref[...]), os_ref)


def _gate_up(x, xs, w_gate, w_up, lead):
    m, k = x.shape
    ms = xs.shape[-2]
    f = w_gate.shape[-1]
    tm = min(ROW_TILE, m)
    tn = COL_TILE
    assert m % tm == 0 and f % tn == 0
    w_spec = pl.BlockSpec((None, None, k, tn), lambda i, j: lead + (0, j))
    return pl.pallas_call(
        _gate_up_kernel,
        out_shape=(jax.ShapeDtypeStruct((m, f), BF16), jax.ShapeDtypeStruct((m // tm, ms, f), BF16)),
        grid=(m // tm, f // tn),
        in_specs=[pl.BlockSpec((tm, k), lambda i, j: (i, 0)),
                  _rows_spec(xs, (ms, k), lambda i, j: (0, 0)), w_spec, w_spec],
        out_specs=(pl.BlockSpec((tm, tn), lambda i, j: (i, j)),
                   pl.BlockSpec((None, ms, tn), lambda i, j: (i, 0, j))),
        compiler_params=_params(("arbitrary", "arbitrary")),
        name="ffn_gate_up",
    )(x, xs, w_gate, w_up)


def _gate_up_norm_kernel(x_ref, y_ref, gp_ref, gn_ref, xs_ref, wg_ref, wu_ref, xo_ref, o_ref, os_ref, xn_scr,
                         *, coef, n_tiles, n_slices, rows):
    i = pl.program_id(0)
    j = pl.program_id(1)

    def norm_slice():
        x = x_ref[...] + coef * _rms(y_ref[...].astype(F32), gp_ref[...])
        xo_ref[...] = x
        tile = jnp.minimum(i, n_tiles - 1)
        r0 = pl.multiple_of(jnp.where(i < n_tiles, jnp.minimum(j, n_slices - 1), n_slices - 1) * rows, rows)
        xn_scr[tile % 2, pl.ds(r0, rows), :] = _rms(x, gn_ref[...]).astype(BF16)

    @pl.when(i == 0)
    def _first():
        norm_slice()
        o_ref[...] = jnp.zeros_like(o_ref)
        os_ref[...] = jnp.zeros_like(os_ref)

    @pl.when(i >= 1)
    def _project():
        wg = wg_ref[...].astype(BF16)
        wu = wu_ref[...].astype(BF16)

        def swiglu(x):
            g = jnp.dot(x, wg, preferred_element_type=F32)
            u = jnp.dot(x, wu, preferred_element_type=F32)
            return g * _sigmoid(g) * u

        o_ref[...] = swiglu(xn_scr[(i - 1) % 2]).astype(o_ref.dtype)
        norm_slice()
        _sample_rows(i == 1, lambda: swiglu(xs_ref[...]), os_ref)


def _gate_up_norm(x, y, g_post, g_next, coef, xs, w_gate, w_up, lead):
    m, d = x.shape
    ms = xs.shape[-2]
    f = w_gate.shape[-1]
    tm = min(ROW_TILE, m)
    tn = COL_TILE
    assert m % tm == 0 and f % tn == 0
    n_tiles, n_cols = m // tm, f // tn
    rows = next(r for r in range(16, tm + 1, 16) if tm % r == 0 and tm // r <= n_cols)
    n_slices = tm // rows

    def slice_map(i, j):
        return (jnp.minimum(i, n_tiles - 1) * n_slices + jnp.where(i < n_tiles, jnp.minimum(j, n_slices - 1),
                                                                 n_slices - 1), 0)

    def col_of(i, j):
        return jnp.where(i >= 1, j, 0)

    row_blk = pl.BlockSpec((rows, d), slice_map)
    vec = pl.BlockSpec((1, d), lambda i, j: (0, 0))
    w_spec = pl.BlockSpec((None, None, d, tn), lambda i, j: lead + (0, col_of(i, j)))
    return pl.pallas_call(
        functools.partial(_gate_up_norm_kernel, coef=coef, n_tiles=n_tiles, n_slices=n_slices, rows=rows),
        out_shape=(jax.ShapeDtypeStruct((m, d), F32), jax.ShapeDtypeStruct((m, f), BF16),
                   jax.ShapeDtypeStruct((n_tiles, ms, f), BF16)),
        grid=(n_tiles + 1, n_cols),
        in_specs=[row_blk, _rows_spec(y, (rows, d), slice_map), vec, vec,
                  _rows_spec(xs, (ms, d), lambda i, j: (0, 0)), w_spec, w_spec],
        out_specs=(row_blk,
                   pl.BlockSpec((tm, tn), lambda i, j: (jnp.maximum(i - 1, 0), col_of(i, j))),
                   pl.BlockSpec((None, ms, tn), lambda i, j: (jnp.maximum(i - 1, 0), 0, col_of(i, j)))),
        scratch_shapes=[pltpu.VMEM((2, tm, d), BF16)],
        compiler_params=_params(("arbitrary", "arbitrary")),
        name="ffn_gate_up_norm",
    )(x, y, g_post.reshape(1, d), g_next.reshape(1, d), xs, w_gate, w_up)


def _mlstm_kernel(*refs, chunk, valid, heads, has_state, scale):
    if has_state:
        (q_ref, k_ref, v_ref, o_ref, gc_ref, bias_ref, hg_ref, c0_ref, n0_ref, m0_ref,
         h_ref, c_ref, n_ref, m_ref) = refs
    else:
        (q_ref, k_ref, v_ref, o_ref, gc_ref, bias_ref, hg_ref,
         h_ref, c_ref, n_ref, m_ref) = refs
    hh = pl.program_id(1)
    cc = pl.program_id(2)
    L = chunk

    @pl.when(cc == 0)
    def _init():
        if has_state:
            c_ref[...] = c0_ref[...]
            n_ref[...] = n0_ref[...]
            m_ref[...] = m0_ref[...]
        else:
            c_ref[...] = jnp.zeros_like(c_ref)
            n_ref[...] = jnp.zeros_like(n_ref)
            m_ref[...] = jnp.zeros_like(m_ref)

    gc = gc_ref[...] + bias_ref[...]
    lane = lax.broadcasted_iota(jnp.int32, gc.shape, 1)
    li_col = jnp.sum(jnp.where(lane == hh, gc, 0.0), axis=1, keepdims=True)
    f_col = jnp.sum(jnp.where(lane == hh + heads, gc, 0.0), axis=1, keepdims=True)
    lf_col = jnp.minimum(f_col, 0.0) - jnp.log(1.0 + jnp.exp(-jnp.abs(f_col)))
    if valid < L:
        rid = lax.broadcasted_iota(jnp.int32, (L, 1), 0)
        li_col = jnp.where(rid < valid, li_col, -jnp.inf)
        lf_col = jnp.where(rid < valid, lf_col, 0.0)

    r = lax.broadcasted_iota(jnp.int32, (L, L), 0)
    s = lax.broadcasted_iota(jnp.int32, (L, L), 1)
    causal = s <= r
    li_row = jnp.sum(jnp.where(r == s, li_col, 0.0), axis=0, keepdims=True)
    lf_row = jnp.sum(jnp.where(r == s, lf_col, 0.0), axis=0, keepdims=True)
    b_col = jnp.sum(jnp.where(causal, lf_row, 0.0), axis=1, keepdims=True)
    b_row = jnp.sum(jnp.where(r <= s, lf_col, 0.0), axis=0, keepdims=True)

    m_prev = m_ref[...]
    m_inter = b_col + m_prev
    log_d = jnp.where(causal, b_col - b_row + li_row, -jnp.inf)
    m_t = jnp.maximum(m_inter, jnp.max(log_d, axis=1, keepdims=True))
    decay_mat = jnp.exp(log_d - m_t)
    inter = jnp.exp(m_inter - m_t)

    q = q_ref[...]
    k = k_ref[...]
    v = v_ref[...]
    c_state = c_ref[...]
    n_state = n_ref[...]
    smat = lax.dot_general(q, k, _NT, preferred_element_type=F32) * scale * decay_mat
    q_c = lax.dot_general(q, c_state.astype(BF16), _NT, preferred_element_type=F32)
    num = jnp.dot(smat.astype(BF16), v, preferred_element_type=F32) + inter * q_c
    q_n = jnp.sum(q.astype(F32) * n_state, axis=1, keepdims=True)
    den = jnp.sum(smat, axis=1, keepdims=True) + inter * q_n
    hval = num / jnp.maximum(jnp.abs(den), jnp.exp(-m_t))

    hg = hg_ref[pl.ds(hh, 1), :]
    h_ref[...] = (_rms(hval, hg) * _sigmoid(o_ref[...].astype(F32))).astype(h_ref.dtype)

    b_last = jnp.sum(lf_col, axis=0, keepdims=True)
    log_w = b_last - b_col + li_col
    m_new = jnp.maximum(b_last + m_prev, jnp.max(log_w, axis=0, keepdims=True))
    w_col = jnp.exp(log_w - m_new)
    decay = jnp.exp(b_last + m_prev - m_new)
    k_scaled = k.astype(F32) * scale
    v_w = (v.astype(F32) * w_col).astype(BF16)
    c_ref[...] = decay * c_state + lax.dot_general(v_w, k_scaled.astype(BF16), _TN,
                                                   preferred_element_type=F32)
    n_ref[...] = decay * n_state + jnp.sum(k_scaled * w_col, axis=0, keepdims=True)
    m_ref[...] = m_new


def _mlstm(proj, gates, bias, head_g, state, layer, bsz, seq, valid, chunk):
    heads, dh = head_g.shape
    d = heads * dh
    nc = seq // chunk
    assert seq % chunk == 0
    has_state = state is not None

    def col(part):
        return _rows_spec(proj, (chunk, dh), lambda b, h, c: (b * nc + c, part * heads + h))

    c_spec = pl.BlockSpec((None, None, dh, dh), lambda b, h, c: (b, h, 0, 0))
    n_spec = pl.BlockSpec((None, None, 1, dh), lambda b, h, c: (b, h, 0, 0))
    m_spec = pl.BlockSpec((None, None, 1, 1), lambda b, h, c: (b, h, 0, 0))
    in_specs = [col(0), col(1), col(2), col(3),
                _rows_spec(gates, (chunk, V7X_LANES), lambda b, h, c: (b * nc + c, 0)),
                pl.BlockSpec((1, V7X_LANES), lambda b, h, c: (0, 0)),
                pl.BlockSpec((heads, dh), lambda b, h, c: (0, 0))]
    args = [proj, proj, proj, proj, gates, bias, head_g]
    if has_state:
        c0, n0, m0 = state
        n_layers = c0.shape[0]
        in_specs += [pl.BlockSpec((None, None, None, dh, dh), lambda b, h, c: (layer, b, h, 0, 0)),
                     pl.BlockSpec((None, None, None, 1, dh), lambda b, h, c: (layer, b, h, 0, 0)),
                     pl.BlockSpec((None, None, None, 1, 1), lambda b, h, c: (layer, b, h, 0, 0))]
        args += [c0, n0.reshape(n_layers, bsz, heads, 1, dh), m0.reshape(n_layers, bsz, heads, 1, 1)]
    h, c_new, n_new, m_new = pl.pallas_call(
        functools.partial(_mlstm_kernel, chunk=chunk, valid=valid, heads=heads,
                          has_state=has_state, scale=dh ** -0.5),
        out_shape=(jax.ShapeDtypeStruct((bsz * seq, d), BF16),
                   jax.ShapeDtypeStruct((bsz, heads, dh, dh), F32),
                   jax.ShapeDtypeStruct((bsz, heads, 1, dh), F32),
                   jax.ShapeDtypeStruct((bsz, heads, 1, 1), F32)),
        grid=(bsz, heads, nc),
        in_specs=in_specs,
        out_specs=(pl.BlockSpec((chunk, dh), lambda b, h, c: (b * nc + c, h)),
                   c_spec, n_spec, m_spec),
        compiler_params=_params(("parallel", "parallel", "arbitrary")),
        name="mlstm",
    )(*args)
    return h, c_new, n_new.reshape(bsz, heads, dh), m_new.reshape(bsz, heads)


def _lambda(lam_ref, lam0):
    lp = lam_ref[...]
    return (jnp.exp(jnp.sum(lp[0:1] * lp[1:2], axis=1, keepdims=True))
            - jnp.exp(jnp.sum(lp[2:3] * lp[3:4], axis=1, keepdims=True)) + lam0)


def _alibi_slopes(heads):
    return np.asarray(2.0 ** (-8.0 * np.arange(1, heads + 1) / heads), np.float32)


def _dattn_prompt_kernel(slopes_ref, q_ref, k_ref, v_ref, lam_ref, hg_ref, o_ref, *, tq, nq, dk, lam0):
    hh = pl.program_id(1)
    qi = pl.program_id(2)
    lam = _lambda(lam_ref, lam0)
    slope = slopes_ref[hh]
    hg = hg_ref[pl.ds(hh, 1), :]

    def attend(q0, n_keys):
        q = q_ref[...]
        kb = k_ref[0:n_keys, :]
        vb = v_ref[0:n_keys, :]
        dist = (q0 + lax.broadcasted_iota(jnp.int32, (tq, n_keys), 0)
                - lax.broadcasted_iota(jnp.int32, (tq, n_keys), 1))
        keep = dist >= 0
        bias = (slope * LOG2E) * dist.astype(F32)
        outs = []
        for j in range(2):
            sc = lax.dot_general(q[:, j * dk:(j + 1) * dk], kb[:, j * dk:(j + 1) * dk], _NT,
                                 preferred_element_type=F32) * (dk ** -0.5 * LOG2E)
            sc = jnp.where(keep, sc - bias, -jnp.inf)
            e = jnp.exp2(sc - jnp.max(sc, axis=1, keepdims=True))
            outs.append(jnp.dot(e.astype(BF16), vb, preferred_element_type=F32)
                        / jnp.sum(e, axis=1, keepdims=True))
        o = outs[0] - lam * outs[1]
        o_ref[...] = (_rms(o, hg) * (1.0 - lam0)).astype(o_ref.dtype)

    for i in range(nq):
        pl.when(qi == i)(functools.partial(attend, i * tq, (i + 1) * tq))


def _dattn_prompt(q, k, v, lam_p, head_g, bsz, seq, lam0):
    heads, dv = head_g.shape
    dk = dv // 2
    tq = min(ATTN_Q_TILE, seq)
    nq = seq // tq
    kv_spec = pl.BlockSpec((seq, dv), lambda b, h, i: (b, h))
    q_spec = pl.BlockSpec((tq, dv), lambda b, h, i: (b * nq + i, h))
    return pl.pallas_call(
        functools.partial(_dattn_prompt_kernel, tq=tq, nq=nq, dk=dk, lam0=lam0),
        out_shape=jax.ShapeDtypeStruct(q.shape, BF16),
        grid=(bsz, heads, nq),
        in_specs=[pl.BlockSpec(memory_space=pltpu.SMEM),
                  q_spec, kv_spec, kv_spec,
                  pl.BlockSpec((4, dk), lambda b, h, i: (0, 0)),
                  pl.BlockSpec((heads, dv), lambda b, h, i: (0, 0))],
        out_specs=q_spec,
        compiler_params=_params(("parallel", "parallel", "arbitrary")),
        name="dattn_prompt",
    )(jnp.asarray(_alibi_slopes(heads)), q, k, v, lam_p, head_g)


def _dattn_sample_kernel(pt_ref, qt_ref, *refs, dk, page, n_pages, pages_per_step, td, lam0):
    kc_refs, vc_refs = refs[:pages_per_step], refs[pages_per_step:2 * pages_per_step]
    (kn_ref, vn_ref, apast_ref, anew_ref, slope_ref, lam_ref, gain_ref,
     o_ref, m_scr, l_scr, acc_scr) = refs[2 * pages_per_step:]
    pg = pl.program_id(1)
    n_steps = n_pages // pages_per_step
    n_cols = qt_ref.shape[1]
    n_rows = vn_ref.shape[0]
    past_len = n_pages * page

    def to_col(row):
        eye = (lax.broadcasted_iota(jnp.int32, (n_cols, n_cols), 0)
               == lax.broadcasted_iota(jnp.int32, (n_cols, n_cols), 1))
        return jnp.sum(jnp.where(eye, row, 0.0), axis=1, keepdims=True)

    @pl.when(pg == 0)
    def _init():
        m_scr[...] = jnp.full_like(m_scr, -jnp.inf)
        l_scr[...] = jnp.zeros_like(l_scr)
        acc_scr[...] = jnp.zeros_like(acc_scr)

    def process(k_ref, v_ref, a_ref, page_bias):
        kb = jnp.concatenate([k_ref[pl.ds(0, n_rows, stride=2), :].astype(BF16),
                              k_ref[pl.ds(1, n_rows, stride=2), :].astype(BF16)], axis=1)
        sc = jnp.dot(kb, qt_ref[...], preferred_element_type=F32) * (dk ** -0.5) + a_ref[...] + page_bias
        m_old = m_scr[...]
        m_new = jnp.maximum(m_old, jnp.max(sc, axis=0, keepdims=True))
        alpha = jnp.exp(m_old - m_new)
        p = jnp.exp(sc - m_new)
        l_scr[...] = alpha * l_scr[...] + jnp.sum(p, axis=0, keepdims=True)
        m_scr[...] = m_new
        pv = lax.dot_general(p.astype(BF16), v_ref[...].astype(BF16), _TN, preferred_element_type=F32)
        acc_scr[...] = to_col(alpha) * acc_scr[...] + pv

    @pl.when(pg < n_steps)
    def _past():
        for u in range(pages_per_step):
            key0 = (pg * pages_per_step + u) * page
            process(kc_refs[u], vc_refs[u], apast_ref, slope_ref[...] * (key0 - past_len).astype(F32))

    @pl.when(pg == n_steps)
    def _new():
        process(kn_ref, vn_ref, anew_ref, 0.0)
        lam = _lambda(lam_ref, lam0)
        on = acc_scr[...] / to_col(l_scr[...])
        o = on - lam * pltpu.roll(on, n_cols - td, 0)
        o_ref[...] = _rms(o, gain_ref[...]) * (1.0 - lam0)


def _dattn_sample(q, cache_k, cache_v, page_table, k_new, v_new, lam_p, head_g, td, lam0):
    heads, dv = head_g.shape
    dk = dv // 2
    bd, n_pages = page_table.shape
    n_pool, page = cache_k.shape[:2]
    hw = heads * dv
    q, k_new, v_new = (a[0] if a.ndim == 3 else a for a in (q, k_new, v_new))
    n_real = heads * 2 * td
    n_cols = -(-n_real // V7X_LANES) * V7X_LANES
    n_rows = page * heads
    pps = math.gcd(n_pages, SAMPLE_PAGES_PER_STEP)
    n_steps = n_pages // pps

    col = np.arange(n_cols)
    real = col < n_real
    col_h = np.where(real, col // (2 * td), -1)
    col_j = (col // td) % 2
    col_t = col % td
    qt = q.reshape(bd, td, heads * 2, dk).transpose(0, 3, 2, 1).reshape(bd, dk, n_real)
    qt = jnp.pad(qt, ((0, 0), (0, 0), (0, n_cols - n_real)))
    qt = jnp.concatenate([jnp.where(col_j == j, qt, jnp.zeros_like(qt)) for j in range(2)], axis=1)

    def page_rows(new, width):
        new = jnp.pad(new.reshape(bd, td, hw), ((0, 0), (0, page - td), (0, 0)))
        return new.reshape(-1, width)

    col_slope = np.where(real, _alibi_slopes(heads)[np.maximum(col_h, 0)], 0.0).astype(np.float32)
    row = np.arange(n_rows)
    row_tok, row_h = row // heads, row % heads
    rel = (col_t[None, :] - row_tok[:, None]).astype(np.float32)
    match = (row_h[:, None] == col_h[None, :]) | ~real[None, :]
    a_past = np.where(match, -col_slope[None, :] * rel, -np.inf).astype(np.float32)
    a_new = np.where(match & ((rel >= 0) | ~real[None, :]), -col_slope[None, :] * rel, -np.inf).astype(np.float32)
    gain = jnp.pad(jnp.repeat(head_g, 2 * td, axis=0), ((0, n_cols - n_real), (0, 0)), constant_values=1.0)

    def page_map(u):
        return lambda b, p, pt: (pt[b * n_pages + jnp.minimum(p, n_steps - 1) * pps + u], 0)

    const = lambda b, p, pt: (0, 0)
    once = dict(pipeline_mode=pl.Buffered(1))
    k2d = cache_k.reshape(n_pool * 2 * n_rows, dk)
    v2d = cache_v.reshape(n_pool * n_rows, dv)
    out = pl.pallas_call(
        functools.partial(_dattn_sample_kernel, dk=dk, page=page, n_pages=n_pages, pages_per_step=pps,
                          td=td, lam0=lam0),
        out_shape=jax.ShapeDtypeStruct((bd, n_cols, dv), F32),
        grid_spec=pltpu.PrefetchScalarGridSpec(
            num_scalar_prefetch=1,
            grid=(bd, n_steps + 1),
            in_specs=([pl.BlockSpec((None, 2 * dk, n_cols), lambda b, p, pt: (b, 0, 0))]
                      + [pl.BlockSpec((2 * n_rows, dk), page_map(u)) for u in range(pps)]
                      + [pl.BlockSpec((n_rows, dv), page_map(u)) for u in range(pps)]
                      + [pl.BlockSpec((2 * n_rows, dk), lambda b, p, pt: (b, 0), **once),
                         pl.BlockSpec((n_rows, dv), lambda b, p, pt: (b, 0), **once),
                         pl.BlockSpec((n_rows, n_cols), const, **once),
                         pl.BlockSpec((n_rows, n_cols), const, **once),
                         pl.BlockSpec((1, n_cols), const),
                         pl.BlockSpec((4, dk), const),
                         pl.BlockSpec((n_cols, dv), const)]),
            out_specs=pl.BlockSpec((None, n_cols, dv), lambda b, p, pt: (b, 0, 0)),
            scratch_shapes=[pltpu.VMEM((1, n_cols), F32), pltpu.VMEM((1, n_cols), F32),
                            pltpu.VMEM((n_cols, dv), F32)]),
        compiler_params=_params(("parallel", "arbitrary")),
        name="dattn_sample",
    )(page_table.reshape(-1), qt, *([k2d] * pps), *([v2d] * pps),
      page_rows(k_new, dk), page_rows(v_new, dv),
      jnp.asarray(a_past), jnp.asarray(a_new), jnp.asarray(col_slope.reshape(1, n_cols)), lam_p, gain)
    o = out[:, :n_real].reshape(bd, heads, 2, td, dv)[:, :, 0].transpose(0, 2, 1, 3)
    return o.reshape(bd * td, hw).astype(BF16)


def _lambda_init(layer):
    return 0.8 - 0.6 * math.exp(-0.3 * layer)


def _trunk(xp, xs, bp, tp, bd, td, state, cache_k, cache_v, page_table, p):
    depth = p["norm_g"].shape[0]
    n_a = p["a_w_in"].shape[0]
    d = xp.shape[1]
    heads_a = p["a_head_g"].shape[1]
    hw = p["b_w_q"].shape[2]
    ts = SAMPLE_ROWS
    states = ([], [], [], [], [], [])
    kv_p = kv_s = kv_bf = None
    w_in_t = p["a_w_in"].transpose(0, 2, 1)

    def both(fn, a, b, *args):
        return fn(a, *args), fn(b, *args)

    def pad_seqs(a):
        a = a[0] if a.ndim == 3 else a
        return jnp.pad(a.reshape(bd, td, -1), ((0, 0), (0, ts - td), (0, 0))).reshape(bd * ts, -1)

    def resid(x, xs_, y, ys_, g_post, g_next, coef):
        (x, xn), (xs_, xns) = _resid_norm(x, y, g_post, g_next, coef), _resid_norm(xs_, ys_, g_post, g_next, coef)
        return x, xs_, xn, xns

    def ffn_up(x, xs_, pend, lead):
        y, ys, g_post, g_next, coef = pend
        xs_, xns = _resid_norm(xs_, ys, g_post, g_next, coef)
        x, h, hs = _gate_up_norm(x, y, g_post, g_next, coef, xns, p["ffn_w_gate"], p["ffn_w_up"], lead)
        return x, xs_, h, hs

    def shared_kv(x, xs_):
        xkv, xkvs = both(_rms_cast, x, xs_, p["kv_norm_g"])
        k_p, k_s, k_bf = _matmul(xkv, xkvs, p["w_kv"], (), 0, hw, F32, "kv_k", bf16_copy=True)
        v_p, v_s, v_bf = _matmul(xkv, xkvs, p["w_kv"], (), hw, p["w_kv"].shape[1] - hw, F32, "kv_v",
                                 bf16_copy=True)
        return (k_p, v_p), (k_s, v_s), (k_bf, v_bf)

    pend = None
    for l in range(depth):
        g = p["norm_g"][l]
        if pend is None:
            xn, xns = both(_rms_cast, xp, xs, g[0])
            h, hs = _gate_up(xn, xns, p["ffn_w_gate"], p["ffn_w_up"], (l, 0))
        else:
            xp, xs, h, hs = ffn_up(xp, xs, pend, (l, 0))
        if l == n_a:
            kv_p, kv_s, kv_bf = shared_kv(xp, xs)
        y, ys = _matmul(h, hs, p["ffn_w_down"], (l, 0), 0, d, BF16, "ffn_down")
        xp, xs, xn, xns = resid(xp, xs, y, ys, g[1], g[2], 0.5)
        if l < n_a:
            proj, projs = _matmul(xn, xns, w_in_t, (l,), 0, 4 * d, BF16, "mlstm_in", w_transposed=True)
            w_gates = jnp.pad(w_in_t[l, 4 * d:, :], ((0, V7X_LANES - 2 * heads_a), (0, 0)))
            gates, gatess = _matmul(xn, xns, w_gates, (), 0, V7X_LANES, F32, "mlstm_gates", w_transposed=True)
            bias = jnp.pad(p["a_b_if"][l], (0, V7X_LANES - 2 * heads_a)).reshape(1, V7X_LANES)
            hg = p["a_head_g"][l]
            hm, c_p, n_p, m_p = _mlstm(proj, gates, bias, hg, None, l, bp, tp, tp, min(MLSTM_CHUNK, tp))
            hms, c_s, n_s, m_s = _mlstm(pad_seqs(projs), pad_seqs(gatess), bias, hg, state, l, bd, ts, td, ts)
            hms = hms.reshape(bd, ts, d)[:, :td].reshape(bd * td, d)
            for acc, val in zip(states, (c_p, n_p, m_p, c_s, n_s, m_s)):
                acc.append(val)
            y, ys = _matmul(hm, hms, p["a_w_out"], (l,), 0, d, BF16, "mlstm_out")
        else:
            j = l - n_a
            q, qs = _matmul(xn, xns, p["b_w_q"], (j,), 0, hw, BF16, "dattn_q")
            lam0 = _lambda_init(l)
            o = _dattn_prompt(q, kv_bf[0], kv_bf[1], p["b_lambda"][j], p["b_head_g"][j], bp, tp, lam0)
            os_ = _dattn_sample(qs, cache_k, cache_v, page_table, kv_s[0], kv_s[1],
                                p["b_lambda"][j], p["b_head_g"][j], td, lam0)
            y, ys = _matmul(o, os_, p["b_w_o"], (j,), 0, d, BF16, "dattn_o")
        xp, xs, h, hs = ffn_up(xp, xs, (y, ys, g[3], g[4], 1.0), (l, 1))
        y, ys = _matmul(h, hs, p["ffn_w_down"], (l, 1), 0, d, BF16, "ffn_down")
        if l + 1 < depth:
            pend = (y, ys, g[5], p["norm_g"][l + 1, 0], 0.5)
        else:
            xp, xs, _, _ = resid(xp, xs, y, ys, g[5], None, 0.5)
            if n_a == depth:
                kv_p, kv_s, kv_bf = shared_kv(xp, xs)
    return xp, xs, [jnp.stack(s) for s in states], kv_p, kv_s


def kernel(x_prompt, x_sample, state_C, state_n, state_m, cache_k, cache_v, page_table, norm_g,
           ffn_w_gate, ffn_w_up, ffn_w_down, a_w_in, a_b_if, a_head_g, a_w_out, kv_norm_g, w_kv,
           b_w_q, b_lambda, b_head_g, b_w_o):
    p = dict(norm_g=norm_g, ffn_w_gate=ffn_w_gate, ffn_w_up=ffn_w_up, ffn_w_down=ffn_w_down,
             a_w_in=a_w_in, a_b_if=a_b_if, a_head_g=a_head_g, a_w_out=a_w_out, kv_norm_g=kv_norm_g,
             w_kv=w_kv, b_w_q=b_w_q, b_lambda=b_lambda, b_head_g=b_head_g, b_w_o=b_w_o)
    bp, tp, d = x_prompt.shape
    bd, td, _ = x_sample.shape
    heads_b, dv = b_head_g.shape[1:]
    dk = dv // 2
    assert td <= SAMPLE_ROWS
    y_p, y_s, (c_p, n_p, m_p, c_s, n_s, m_s), (k_p, v_p), (k_s, v_s) = _trunk(
        x_prompt.reshape(bp * tp, d), x_sample.reshape(bd * td, d), bp, tp, bd, td,
        (state_C, state_n, state_m), cache_k, cache_v, page_table, p)
    return (y_p.reshape(bp, tp, d), y_s.reshape(bd, td, d), c_p, n_p, m_p,
            k_p.reshape(bp, tp, heads_b, 2, dk), v_p.reshape(bp, tp, heads_b, dv),
            c_s, n_s, m_s,
            k_s[0].reshape(bd, td, heads_b, 2, dk), v_s[0].reshape(bd, td, heads_b, dv))
```

```python
import functools
import math

import numpy as np
import jax
import jax.numpy as jnp
from jax import lax
from jax.experimental import pallas as pl
from jax.experimental.pallas import tpu as pltpu

EPS = 1e-6
LOG2E = math.log2(math.e)
BF16 = jnp.bfloat16
F32 = jnp.float32

V7X_LANES = 128
V7X_MXU_COLS = 256
V7X_VMEM_BYTES = 64 * 1024 * 1024
VMEM_LIMIT_BYTES = V7X_VMEM_BYTES - 6 * 1024 * 1024

SAMPLE_ROWS = 16
ROW_TILE = 1024
COL_TILE = V7X_MXU_COLS
NORM_ROWS = 256
NORM_BLOCK_ROWS = 128
MLSTM_CHUNK = 256
ATTN_Q_TILE = 256
SOFTMAX_STRIP = 16
SAMPLE_PAGES_PER_STEP = 4

_NT = (((1,), (1,)), ((), ()))
_TN = (((0,), (0,)), ((), ()))
_NN = (((1,), (0,)), ((), ()))


def _params(semantics):
    return pltpu.CompilerParams(dimension_semantics=semantics, vmem_limit_bytes=VMEM_LIMIT_BYTES)


def _rows_spec(arr, block, index_map, **kwargs):
    if arr.ndim == 3:
        return pl.BlockSpec((None,) + block, lambda *g: (0,) + index_map(*g), **kwargs)
    return pl.BlockSpec(block, index_map, **kwargs)


def _rms(x, g):
    return x * lax.rsqrt(jnp.mean(x * x, axis=-1, keepdims=True) + EPS) * g


def _sigmoid(x):
    return 1.0 / (1.0 + jnp.exp(-x))


def _rms_cast_kernel(x_ref, g_ref, o_ref):
    o_ref[...] = _rms(x_ref[...], g_ref[...]).astype(o_ref.dtype)


def _rms_cast(x, g):
    m, d = x.shape
    tr = min(NORM_ROWS, m)
    return pl.pallas_call(
        _rms_cast_kernel,
        out_shape=jax.ShapeDtypeStruct((m, d), BF16),
        grid=(m // tr,),
        in_specs=[pl.BlockSpec((tr, d), lambda i: (i, 0)),
                  pl.BlockSpec((1, d), lambda i: (0, 0))],
        out_specs=pl.BlockSpec((tr, d), lambda i: (i, 0)),
        compiler_params=_params(("parallel",)),
        name="rms_cast",
    )(x, g.reshape(1, d))


def _resid_norm_kernel(x_ref, y_ref, gp_ref, gn_ref, xo_ref, xn_ref, *, coef):
    x = x_ref[...] + coef * _rms(y_ref[...].astype(F32), gp_ref[...])
    xo_ref[...] = x
    xn_ref[...] = _rms(x, gn_ref[...]).astype(xn_ref.dtype)


def _resid_kernel(x_ref, y_ref, gp_ref, xo_ref, *, coef):
    xo_ref[...] = x_ref[...] + coef * _rms(y_ref[...].astype(F32), gp_ref[...])


def _resid_norm(x, y, g_post, g_next, coef):
    m, d = x.shape
    tr = min(NORM_ROWS, m)
    row = pl.BlockSpec((tr, d), lambda i: (i, 0))
    y_row = _rows_spec(y, (tr, d), lambda i: (i, 0))
    vec = pl.BlockSpec((1, d), lambda i: (0, 0))
    if g_next is None:
        return pl.pallas_call(
            functools.partial(_resid_kernel, coef=coef),
            out_shape=jax.ShapeDtypeStruct((m, d), F32),
            grid=(m // tr,),
            in_specs=[row, y_row, vec],
            out_specs=row,
            compiler_params=_params(("parallel",)),
            name="resid",
        )(x, y, g_post.reshape(1, d)), None
    return pl.pallas_call(
        functools.partial(_resid_norm_kernel, coef=coef),
        out_shape=(jax.ShapeDtypeStruct((m, d), F32), jax.ShapeDtypeStruct((m, d), BF16)),
        grid=(m // tr,),
        in_specs=[row, y_row, vec, vec],
        out_specs=(row, row),
        compiler_params=_params(("parallel",)),
        name="resid_norm",
    )(x, y, g_post.reshape(1, d), g_next.reshape(1, d))


def _sample_rows(first_row_tile, compute, os_ref):
    @pl.when(first_row_tile)
    def _():
        os_ref[...] = compute().astype(os_ref.dtype)

    @pl.when(jnp.logical_not(first_row_tile))
    def _():
        os_ref[...] = jnp.zeros_like(os_ref)


def _mm_kernel(x_ref, xs_ref, w_ref, *out_refs, w_dims):
    o_ref, os_ref = out_refs[:2]
    w = w_ref[...].astype(BF16)
    y = lax.dot_general(x_ref[...], w, w_dims, preferred_element_type=F32)
    o_ref[...] = y.astype(o_ref.dtype)
    if len(out_refs) == 3:
        out_refs[2][...] = y.astype(BF16)
    _sample_rows(pl.program_id(0) == 0,
                 lambda: lax.dot_general(xs_ref[...], w, w_dims, preferred_element_type=F32), os_ref)


def _col_tile(k, n):
    tn = min(COL_TILE, n)
    while n % (2 * tn) == 0 and k * 2 * tn * 4 <= (8 << 20):
        tn *= 2
    return tn


def _matmul(x, xs, w, lead, n0, n, out_dtype, name, w_transposed=False, bf16_copy=False):
    m, k = x.shape
    ms = xs.shape[-2]
    tm = min(ROW_TILE, m)
    tn = _col_tile(k, n)
    assert m % tm == 0 and n % tn == 0 and n0 % tn == 0
    jb0 = n0 // tn
    nlead = len(lead)
    if w_transposed:
        w_spec = pl.BlockSpec((None,) * nlead + (tn, k), lambda i, j: lead + (j + jb0, 0))
    else:
        w_spec = pl.BlockSpec((None,) * nlead + (k, tn), lambda i, j: lead + (0, j + jb0))
    x_mode = dict(pipeline_mode=pl.Buffered(1)) if tm * k * 2 > (8 << 20) else {}
    tile = pl.BlockSpec((tm, tn), lambda i, j: (i, j))
    out_shape = [jax.ShapeDtypeStruct((m, n), out_dtype), jax.ShapeDtypeStruct((m // tm, ms, n), out_dtype)]
    out_specs = [tile, pl.BlockSpec((None, ms, tn), lambda i, j: (i, 0, j))]
    if bf16_copy:
        out_shape.append(jax.ShapeDtypeStruct((m, n), BF16))
        out_specs.append(tile)
    outs = pl.pallas_call(
        functools.partial(_mm_kernel, w_dims=_NT if w_transposed else _NN),
        out_shape=tuple(out_shape),
        grid=(m // tm, n // tn),
        in_specs=[pl.BlockSpec((tm, k), lambda i, j: (i, 0), **x_mode),
                  _rows_spec(xs, (ms, k), lambda i, j: (0, 0), **x_mode),
                  w_spec],
        out_specs=tuple(out_specs),
        compiler_params=_params(("arbitrary", "arbitrary")),
        name=name,
    )(x, xs, w)
    return outs


def _gate_up_kernel(x_ref, xs_ref, wg_ref, wu_ref, o_ref, os_ref):
    wg = wg_ref[...].astype(BF16)
    wu = wu_ref[...].astype(BF16)

    def swiglu(x):
        g = jnp.dot(x, wg, preferred_element_type=F32)
        u = jnp.dot(x, wu, preferred_element_type=F32)
        return g * _sigmoid(g) * u

    o_ref[...] = swiglu(x_ref[...]).astype(o_ref.dtype)
    _sample_rows(pl.program_id(0) == 0, lambda: swiglu(xs_ref[...]), os_ref)


def _gate_up(x, xs, w_gate, w_up, lead):
    m, k = x.shape
    ms = xs.shape[-2]
    f = w_gate.shape[-1]
    tm = min(ROW_TILE, m)
    tn = COL_TILE
    assert m % tm == 0 and f % tn == 0
    w_spec = pl.BlockSpec((None, None, k, tn), lambda i, j: lead + (0, j))
    return pl.pallas_call(
        _gate_up_kernel,
        out_shape=(jax.ShapeDtypeStruct((m, f), BF16), jax.ShapeDtypeStruct((m // tm, ms, f), BF16)),
        grid=(m // tm, f // tn),
        in_specs=[pl.BlockSpec((tm, k), lambda i, j: (i, 0)),
                  _rows_spec(xs, (ms, k), lambda i, j: (0, 0)), w_spec, w_spec],
        out_specs=(pl.BlockSpec((tm, tn), lambda i, j: (i, j)),
                   pl.BlockSpec((None, ms, tn), lambda i, j: (i, 0, j))),
        compiler_params=_params(("arbitrary", "arbitrary")),
        name="ffn_gate_up",
    )(x, xs, w_gate, w_up)


def _gate_up_norm_kernel(x_ref, y_ref, gp_ref, gn_ref, xs_ref, wg_ref, wu_ref, xo_ref, o_ref, os_ref, xn_scr,
                         *, coef, n_tiles, n_slices, rows):
    i = pl.program_id(0)
    j = pl.program_id(1)
    blk = x_ref.shape[0]
    n_sub = blk // rows
    n_blocks = n_slices // n_sub

    def update(rows_x, rows_y):
        x = rows_x + coef * _rms(rows_y.astype(F32), gp_ref[...])
        return x, _rms(x, gn_ref[...]).astype(BF16)

    @pl.when(jnp.logical_and(i == 0, j < n_blocks))
    def _first_tile():
        x, xn = update(x_ref[...], y_ref[...])
        xo_ref[...] = x
        xn_scr[0, pl.ds(pl.multiple_of(j * blk, blk), blk), :] = xn

    @pl.when(i == 0)
    def _parked_outputs():
        o_ref[...] = jnp.zeros_like(o_ref)
        os_ref[...] = jnp.zeros_like(os_ref)

    @pl.when(i >= 1)
    def _project():
        wg = wg_ref[...].astype(BF16)
        wu = wu_ref[...].astype(BF16)

        def swiglu(x):
            g = jnp.dot(x, wg, preferred_element_type=F32)
            u = jnp.dot(x, wu, preferred_element_type=F32)
            return g * _sigmoid(g) * u

        o_ref[...] = swiglu(xn_scr[(i - 1) % 2]).astype(o_ref.dtype)
        s = jnp.where(i < n_tiles, jnp.minimum(j, n_slices - 1), n_slices - 1)
        sub = pl.ds(pl.multiple_of((s % n_sub) * rows, rows), rows)
        x, xn = update(x_ref[sub, :], y_ref[sub, :])
        xo_ref[sub, :] = x
        xn_scr[jnp.minimum(i, n_tiles - 1) % 2, pl.ds(pl.multiple_of(s * rows, rows), rows), :] = xn
        _sample_rows(i == 1, lambda: swiglu(xs_ref[...]), os_ref)


def _gate_up_norm(x, y, g_post, g_next, coef, xs, w_gate, w_up, lead):
    m, d = x.shape
    ms = xs.shape[-2]
    f = w_gate.shape[-1]
    tm = min(ROW_TILE, m)
    tn = COL_TILE
    assert m % tm == 0 and f % tn == 0
    n_tiles, n_cols = m // tm, f // tn
    rows = next(r for r in range(16, tm + 1, 16) if tm % r == 0 and tm // r <= n_cols)
    n_slices = tm // rows
    blk = max(rows, min(NORM_BLOCK_ROWS, tm))
    assert blk % rows == 0 and tm % blk == 0
    n_sub, n_blocks = blk // rows, tm // blk

    def block_map(i, j):
        in_tile = jnp.where(i == 0, jnp.minimum(j, n_blocks - 1),
                            jnp.where(i < n_tiles, jnp.minimum(j, n_slices - 1) // n_sub, n_blocks - 1))
        return (jnp.minimum(i, n_tiles - 1) * n_blocks + in_tile, 0)

    def col_of(i, j):
        return jnp.where(i >= 1, j, 0)

    row_blk = pl.BlockSpec((blk, d), block_map)
    vec = pl.BlockSpec((1, d), lambda i, j: (0, 0))
    w_spec = pl.BlockSpec((None, None, d, tn), lambda i, j: lead + (0, col_of(i, j)))
    return pl.pallas_call(
        functools.partial(_gate_up_norm_kernel, coef=coef, n_tiles=n_tiles, n_slices=n_slices, rows=rows),
        out_shape=(jax.ShapeDtypeStruct((m, d), F32), jax.ShapeDtypeStruct((m, f), BF16),
                   jax.ShapeDtypeStruct((n_tiles, ms, f), BF16)),
        grid=(n_tiles + 1, n_cols),
        in_specs=[row_blk, _rows_spec(y, (blk, d), block_map), vec, vec,
                  _rows_spec(xs, (ms, d), lambda i, j: (0, 0)), w_spec, w_spec],
        out_specs=(row_blk,
                   pl.BlockSpec((tm, tn), lambda i, j: (jnp.maximum(i - 1, 0), col_of(i, j))),
                   pl.BlockSpec((None, ms, tn), lambda i, j: (jnp.maximum(i - 1, 0), 0, col_of(i, j)))),
        scratch_shapes=[pltpu.VMEM((2, tm, d), BF16)],
        compiler_params=_params(("arbitrary", "arbitrary")),
        name="ffn_gate_up_norm",
    )(x, y, g_post.reshape(1, d), g_next.reshape(1, d), xs, w_gate, w_up)


def _mlstm_kernel(*refs, chunk, valid, heads, has_state, scale):
    if has_state:
        (q_ref, k_ref, v_ref, o_ref, gc_ref, bias_ref, hg_ref, c0_ref, n0_ref, m0_ref,
         h_ref, c_ref, n_ref, m_ref) = refs
    else:
        (q_ref, k_ref, v_ref, o_ref, gc_ref, bias_ref, hg_ref,
         h_ref, c_ref, n_ref, m_ref) = refs
    hh = pl.program_id(1)
    cc = pl.program_id(2)
    L = chunk

    @pl.when(cc == 0)
    def _init():
        if has_state:
            c_ref[...] = c0_ref[...]
            n_ref[...] = n0_ref[...]
            m_ref[...] = m0_ref[...]
        else:
            c_ref[...] = jnp.zeros_like(c_ref)
            n_ref[...] = jnp.zeros_like(n_ref)
            m_ref[...] = jnp.zeros_like(m_ref)

    gc = gc_ref[...] + bias_ref[...]
    lane = lax.broadcasted_iota(jnp.int32, gc.shape, 1)
    li_col = jnp.sum(jnp.where(lane == hh, gc, 0.0), axis=1, keepdims=True)
    f_col = jnp.sum(jnp.where(lane == hh + heads, gc, 0.0), axis=1, keepdims=True)
    lf_col = jnp.minimum(f_col, 0.0) - jnp.log(1.0 + jnp.exp(-jnp.abs(f_col)))
    if valid < L:
        rid = lax.broadcasted_iota(jnp.int32, (L, 1), 0)
        li_col = jnp.where(rid < valid, li_col, -jnp.inf)
        lf_col = jnp.where(rid < valid, lf_col, 0.0)

    r = lax.broadcasted_iota(jnp.int32, (L, L), 0)
    s = lax.broadcasted_iota(jnp.int32, (L, L), 1)
    causal = s <= r
    li_row = jnp.sum(jnp.where(r == s, li_col, 0.0), axis=0, keepdims=True)
    lf_row = jnp.sum(jnp.where(r == s, lf_col, 0.0), axis=0, keepdims=True)
    b_col = jnp.sum(jnp.where(causal, lf_row, 0.0), axis=1, keepdims=True)
    b_row = jnp.sum(jnp.where(r <= s, lf_col, 0.0), axis=0, keepdims=True)

    m_prev = m_ref[...]
    m_inter = b_col + m_prev
    log_d = jnp.where(causal, b_col - b_row + li_row, -jnp.inf)
    m_t = jnp.maximum(m_inter, jnp.max(log_d, axis=1, keepdims=True))
    decay_mat = jnp.exp(log_d - m_t)
    inter = jnp.exp(m_inter - m_t)

    q = q_ref[...]
    k = k_ref[...]
    v = v_ref[...]
    c_state = c_ref[...]
    n_state = n_ref[...]
    smat = lax.dot_general(q, k, _NT, preferred_element_type=F32) * scale * decay_mat
    q_c = lax.dot_general(q, c_state.astype(BF16), _NT, preferred_element_type=F32)
    num = jnp.dot(smat.astype(BF16), v, preferred_element_type=F32) + inter * q_c
    q_n = jnp.sum(q.astype(F32) * n_state, axis=1, keepdims=True)
    den = jnp.sum(smat, axis=1, keepdims=True) + inter * q_n
    hval = num / jnp.maximum(jnp.abs(den), jnp.exp(-m_t))

    hg = hg_ref[pl.ds(hh, 1), :]
    h_ref[...] = (_rms(hval, hg) * _sigmoid(o_ref[...].astype(F32))).astype(h_ref.dtype)

    b_last = jnp.sum(lf_col, axis=0, keepdims=True)
    log_w = b_last - b_col + li_col
    m_new = jnp.maximum(b_last + m_prev, jnp.max(log_w, axis=0, keepdims=True))
    w_col = jnp.exp(log_w - m_new)
    decay = jnp.exp(b_last + m_prev - m_new)
    k_scaled = k.astype(F32) * scale
    v_w = (v.astype(F32) * w_col).astype(BF16)
    c_ref[...] = decay * c_state + lax.dot_general(v_w, k_scaled.astype(BF16), _TN,
                                                   preferred_element_type=F32)
    n_ref[...] = decay * n_state + jnp.sum(k_scaled * w_col, axis=0, keepdims=True)
    m_ref[...] = m_new


def _mlstm(proj, gates, bias, head_g, state, layer, bsz, seq, valid, chunk):
    heads, dh = head_g.shape
    d = heads * dh
    nc = seq // chunk
    assert seq % chunk == 0
    has_state = state is not None

    def col(part):
        return _rows_spec(proj, (chunk, dh), lambda b, h, c: (b * nc + c, part * heads + h))

    c_spec = pl.BlockSpec((None, None, dh, dh), lambda b, h, c: (b, h, 0, 0))
    n_spec = pl.BlockSpec((None, None, 1, dh), lambda b, h, c: (b, h, 0, 0))
    m_spec = pl.BlockSpec((None, None, 1, 1), lambda b, h, c: (b, h, 0, 0))
    in_specs = [col(0), col(1), col(2), col(3),
                _rows_spec(gates, (chunk, V7X_LANES), lambda b, h, c: (b * nc + c, 0)),
                pl.BlockSpec((1, V7X_LANES), lambda b, h, c: (0, 0)),
                pl.BlockSpec((heads, dh), lambda b, h, c: (0, 0))]
    args = [proj, proj, proj, proj, gates, bias, head_g]
    if has_state:
        c0, n0, m0 = state
        n_layers = c0.shape[0]
        in_specs += [pl.BlockSpec((None, None, None, dh, dh), lambda b, h, c: (layer, b, h, 0, 0)),
                     pl.BlockSpec((None, None, None, 1, dh), lambda b, h, c: (layer, b, h, 0, 0)),
                     pl.BlockSpec((None, None, None, 1, 1), lambda b, h, c: (layer, b, h, 0, 0))]
        args += [c0, n0.reshape(n_layers, bsz, heads, 1, dh), m0.reshape(n_layers, bsz, heads, 1, 1)]
    h, c_new, n_new, m_new = pl.pallas_call(
        functools.partial(_mlstm_kernel, chunk=chunk, valid=valid, heads=heads,
                          has_state=has_state, scale=dh ** -0.5),
        out_shape=(jax.ShapeDtypeStruct((bsz * seq, d), BF16),
                   jax.ShapeDtypeStruct((bsz, heads, dh, dh), F32),
                   jax.ShapeDtypeStruct((bsz, heads, 1, dh), F32),
                   jax.ShapeDtypeStruct((bsz, heads, 1, 1), F32)),
        grid=(bsz, heads, nc),
        in_specs=in_specs,
        out_specs=(pl.BlockSpec((chunk, dh), lambda b, h, c: (b * nc + c, h)),
                   c_spec, n_spec, m_spec),
        compiler_params=_params(("parallel", "parallel", "arbitrary")),
        name="mlstm",
    )(*args)
    return h, c_new, n_new.reshape(bsz, heads, dh), m_new.reshape(bsz, heads)


def _lambda(lam_ref, lam0):
    lp = lam_ref[...]
    return (jnp.exp(jnp.sum(lp[0:1] * lp[1:2], axis=1, keepdims=True))
            - jnp.exp(jnp.sum(lp[2:3] * lp[3:4], axis=1, keepdims=True)) + lam0)


def _alibi_slopes(heads):
    return np.asarray(2.0 ** (-8.0 * np.arange(1, heads + 1) / heads), np.float32)


def _dattn_prompt_kernel(slopes_ref, q_ref, k_ref, v_ref, lam_ref, hg_ref, o_ref,
                         s_scr, p_scr, cb_scr, l_scr, *, tq, nq, dk, lam0):
    hh = pl.program_id(1)
    qi = pl.program_id(2)
    lam = _lambda(lam_ref, lam0)
    slope = slopes_ref[hh]
    hg = hg_ref[pl.ds(hh, 1), :]
    n_strips = tq // SOFTMAX_STRIP

    def attend(n_keys):
        q = q_ref[...]
        n_past = n_keys - tq
        key_pos = lax.broadcasted_iota(jnp.int32, (SOFTMAX_STRIP, n_keys), 1).astype(F32)
        cb_scr[:, 0:n_keys] = (slope * LOG2E) * key_pos
        for j in range(2):
            s_scr[j, :, 0:n_keys] = lax.dot_general(q[:, j * dk:(j + 1) * dk],
                                                    k_ref[0:n_keys, j * dk:(j + 1) * dk], _NT,
                                                    preferred_element_type=F32)

        def strip(r, carry):
            r0 = pl.multiple_of(r * SOFTMAX_STRIP, SOFTMAX_STRIP)
            keep = (lax.broadcasted_iota(jnp.int32, (SOFTMAX_STRIP, tq), 1)
                    <= r0 + lax.broadcasted_iota(jnp.int32, (SOFTMAX_STRIP, tq), 0))
            for j in range(2):
                t = s_scr[j, pl.ds(r0, SOFTMAX_STRIP), 0:n_keys] * (dk ** -0.5 * LOG2E) + cb_scr[:, 0:n_keys]
                own = jnp.where(keep, t[:, n_past:], -jnp.inf)
                t = jnp.concatenate([t[:, :n_past], own], axis=1) if n_past else own
                e = jnp.exp2(t - jnp.max(t, axis=1, keepdims=True))
                l_scr[j, pl.ds(r0, SOFTMAX_STRIP), :] = jnp.sum(e, axis=1, keepdims=True)
                p_scr[j, pl.ds(r0, SOFTMAX_STRIP), 0:n_keys] = e.astype(BF16)
            return carry

        lax.fori_loop(0, n_strips, strip, 0, unroll=True)
        outs = [jnp.dot(p_scr[j, :, 0:n_keys], v_ref[0:n_keys, :], preferred_element_type=F32) / l_scr[j]
                for j in range(2)]
        o = outs[0] - lam * outs[1]
        o_ref[...] = (_rms(o, hg) * (1.0 - lam0)).astype(o_ref.dtype)

    for i in range(nq):
        pl.when(qi == i)(functools.partial(attend, (i + 1) * tq))


def _dattn_prompt(q, k, v, lam_p, head_g, bsz, seq, lam0):
    heads, dv = head_g.shape
    dk = dv // 2
    tq = min(ATTN_Q_TILE, seq)
    nq = seq // tq
    kv_spec = pl.BlockSpec((seq, dv), lambda b, h, i: (b, h))
    q_spec = pl.BlockSpec((tq, dv), lambda b, h, i: (b * nq + i, h))
    return pl.pallas_call(
        functools.partial(_dattn_prompt_kernel, tq=tq, nq=nq, dk=dk, lam0=lam0),
        out_shape=jax.ShapeDtypeStruct(q.shape, BF16),
        grid=(bsz, heads, nq),
        in_specs=[pl.BlockSpec(memory_space=pltpu.SMEM),
                  q_spec, kv_spec, kv_spec,
                  pl.BlockSpec((4, dk), lambda b, h, i: (0, 0)),
                  pl.BlockSpec((heads, dv), lambda b, h, i: (0, 0))],
        out_specs=q_spec,
        scratch_shapes=[pltpu.VMEM((2, tq, seq), F32), pltpu.VMEM((2, tq, seq), BF16),
                        pltpu.VMEM((SOFTMAX_STRIP, seq), F32), pltpu.VMEM((2, tq, 1), F32)],
        compiler_params=_params(("parallel", "parallel", "arbitrary")),
        name="dattn_prompt",
    )(jnp.asarray(_alibi_slopes(heads)), q, k, v, lam_p, head_g)


def _dattn_sample_kernel(pt_ref, qt_ref, *refs, dk, page, n_pages, pages_per_step, td, lam0):
    kc_refs, vc_refs = refs[:pages_per_step], refs[pages_per_step:2 * pages_per_step]
    (kn_ref, vn_ref, apast_ref, anew_ref, slope_ref, lam_ref, gain_ref,
     o_ref, m_scr, l_scr, acc_scr) = refs[2 * pages_per_step:]
    pg = pl.program_id(1)
    n_steps = n_pages // pages_per_step
    n_cols = qt_ref.shape[1]
    n_rows = vn_ref.shape[0]
    past_len = n_pages * page

    def to_col(row):
        eye = (lax.broadcasted_iota(jnp.int32, (n_cols, n_cols), 0)
               == lax.broadcasted_iota(jnp.int32, (n_cols, n_cols), 1))
        return jnp.sum(jnp.where(eye, row, 0.0), axis=1, keepdims=True)

    @pl.when(pg == 0)
    def _init():
        m_scr[...] = jnp.full_like(m_scr, -jnp.inf)
        l_scr[...] = jnp.zeros_like(l_scr)
        acc_scr[...] = jnp.zeros_like(acc_scr)

    def process(k_ref, v_ref, a_ref, page_bias):
        kb = jnp.concatenate([k_ref[pl.ds(0, n_rows, stride=2), :].astype(BF16),
                              k_ref[pl.ds(1, n_rows, stride=2), :].astype(BF16)], axis=1)
        sc = jnp.dot(kb, qt_ref[...], preferred_element_type=F32) * (dk ** -0.5) + a_ref[...] + page_bias
        m_old = m_scr[...]
        m_new = jnp.maximum(m_old, jnp.max(sc, axis=0, keepdims=True))
        alpha = jnp.exp(m_old - m_new)
        p = jnp.exp(sc - m_new)
        l_scr[...] = alpha * l_scr[...] + jnp.sum(p, axis=0, keepdims=True)
        m_scr[...] = m_new
        pv = lax.dot_general(p.astype(BF16), v_ref[...].astype(BF16), _TN, preferred_element_type=F32)
        acc_scr[...] = to_col(alpha) * acc_scr[...] + pv

    @pl.when(pg < n_steps)
    def _past():
        for u in range(pages_per_step):
            key0 = (pg * pages_per_step + u) * page
            process(kc_refs[u], vc_refs[u], apast_ref, slope_ref[...] * (key0 - past_len).astype(F32))

    @pl.when(pg == n_steps)
    def _new():
        process(kn_ref, vn_ref, anew_ref, 0.0)
        lam = _lambda(lam_ref, lam0)
        on = acc_scr[...] / to_col(l_scr[...])
        o = on - lam * pltpu.roll(on, n_cols - td, 0)
        o_ref[...] = _rms(o, gain_ref[...]) * (1.0 - lam0)


def _dattn_sample(q, cache_k, cache_v, page_table, k_new, v_new, lam_p, head_g, td, lam0):
    heads, dv = head_g.shape
    dk = dv // 2
    bd, n_pages = page_table.shape
    n_pool, page = cache_k.shape[:2]
    hw = heads * dv
    q, k_new, v_new = (a[0] if a.ndim == 3 else a for a in (q, k_new, v_new))
    n_real = heads * 2 * td
    n_cols = -(-n_real // V7X_LANES) * V7X_LANES
    n_rows = page * heads
    pps = math.gcd(n_pages, SAMPLE_PAGES_PER_STEP)
    n_steps = n_pages // pps

    col = np.arange(n_cols)
    real = col < n_real
    col_h = np.where(real, col // (2 * td), -1)
    col_j = (col // td) % 2
    col_t = col % td
    qt = q.reshape(bd, td, heads * 2, dk).transpose(0, 3, 2, 1).reshape(bd, dk, n_real)
    qt = jnp.pad(qt, ((0, 0), (0, 0), (0, n_cols - n_real)))
    qt = jnp.concatenate([jnp.where(col_j == j, qt, jnp.zeros_like(qt)) for j in range(2)], axis=1)

    def page_rows(new, width):
        new = jnp.pad(new.reshape(bd, td, hw), ((0, 0), (0, page - td), (0, 0)))
        return new.reshape(-1, width)

    col_slope = np.where(real, _alibi_slopes(heads)[np.maximum(col_h, 0)], 0.0).astype(np.float32)
    row = np.arange(n_rows)
    row_tok, row_h = row // heads, row % heads
    rel = (col_t[None, :] - row_tok[:, None]).astype(np.float32)
    match = (row_h[:, None] == col_h[None, :]) | ~real[None, :]
    a_past = np.where(match, -col_slope[None, :] * rel, -np.inf).astype(np.float32)
    a_new = np.where(match & ((rel >= 0) | ~real[None, :]), -col_slope[None, :] * rel, -np.inf).astype(np.float32)
    gain = jnp.pad(jnp.repeat(head_g, 2 * td, axis=0), ((0, n_cols - n_real), (0, 0)), constant_values=1.0)

    def page_map(u):
        return lambda b, p, pt: (pt[b * n_pages + jnp.minimum(p, n_steps - 1) * pps + u], 0)

    const = lambda b, p, pt: (0, 0)
    once = dict(pipeline_mode=pl.Buffered(1))
    k2d = cache_k.reshape(n_pool * 2 * n_rows, dk)
    v2d = cache_v.reshape(n_pool * n_rows, dv)
    out = pl.pallas_call(
        functools.partial(_dattn_sample_kernel, dk=dk, page=page, n_pages=n_pages, pages_per_step=pps,
                          td=td, lam0=lam0),
        out_shape=jax.ShapeDtypeStruct((bd, n_cols, dv), F32),
        grid_spec=pltpu.PrefetchScalarGridSpec(
            num_scalar_prefetch=1,
            grid=(bd, n_steps + 1),
            in_specs=([pl.BlockSpec((None, 2 * dk, n_cols), lambda b, p, pt: (b, 0, 0))]
                      + [pl.BlockSpec((2 * n_rows, dk), page_map(u)) for u in range(pps)]
                      + [pl.BlockSpec((n_rows, dv), page_map(u)) for u in range(pps)]
                      + [pl.BlockSpec((2 * n_rows, dk), lambda b, p, pt: (b, 0), **once),
                         pl.BlockSpec((n_rows, dv), lambda b, p, pt: (b, 0), **once),
                         pl.BlockSpec((n_rows, n_cols), const, **once),
                         pl.BlockSpec((n_rows, n_cols), const, **once),
                         pl.BlockSpec((1, n_cols), const),
                         pl.BlockSpec((4, dk), const),
                         pl.BlockSpec((n_cols, dv), const)]),
            out_specs=pl.BlockSpec((None, n_cols, dv), lambda b, p, pt: (b, 0, 0)),
            scratch_shapes=[pltpu.VMEM((1, n_cols), F32), pltpu.VMEM((1, n_cols), F32),
                            pltpu.VMEM((n_cols, dv), F32)]),
        compiler_params=_params(("parallel", "arbitrary")),
        name="dattn_sample",
    )(page_table.reshape(-1), qt, *([k2d] * pps), *([v2d] * pps),
      page_rows(k_new, dk), page_rows(v_new, dv),
      jnp.asarray(a_past), jnp.asarray(a_new), jnp.asarray(col_slope.reshape(1, n_cols)), lam_p, gain)
    o = out[:, :n_real].reshape(bd, heads, 2, td, dv)[:, :, 0].transpose(0, 2, 1, 3)
    return o.reshape(bd * td, hw).astype(BF16)


def _lambda_init(layer):
    return 0.8 - 0.6 * math.exp(-0.3 * layer)


def _trunk(xp, xs, bp, tp, bd, td, state, cache_k, cache_v, page_table, p):
    depth = p["norm_g"].shape[0]
    n_a = p["a_w_in"].shape[0]
    d = xp.shape[1]
    heads_a = p["a_head_g"].shape[1]
    hw = p["b_w_q"].shape[2]
    ts = SAMPLE_ROWS
    states = ([], [], [], [], [], [])
    kv_p = kv_s = kv_bf = None
    w_in_t = p["a_w_in"].transpose(0, 2, 1)

    def both(fn, a, b, *args):
        return fn(a, *args), fn(b, *args)

    def pad_seqs(a):
        a = a[0] if a.ndim == 3 else a
        return jnp.pad(a.reshape(bd, td, -1), ((0, 0), (0, ts - td), (0, 0))).reshape(bd * ts, -1)

    def resid(x, xs_, y, ys_, g_post, g_next, coef):
        (x, xn), (xs_, xns) = _resid_norm(x, y, g_post, g_next, coef), _resid_norm(xs_, ys_, g_post, g_next, coef)
        return x, xs_, xn, xns

    def ffn_up(x, xs_, pend, lead):
        y, ys, g_post, g_next, coef = pend
        xs_, xns = _resid_norm(xs_, ys, g_post, g_next, coef)
        x, h, hs = _gate_up_norm(x, y, g_post, g_next, coef, xns, p["ffn_w_gate"], p["ffn_w_up"], lead)
        return x, xs_, h, hs

    def shared_kv(x, xs_):
        xkv, xkvs = both(_rms_cast, x, xs_, p["kv_norm_g"])
        k_p, k_s, k_bf = _matmul(xkv, xkvs, p["w_kv"], (), 0, hw, F32, "kv_k", bf16_copy=True)
        v_p, v_s, v_bf = _matmul(xkv, xkvs, p["w_kv"], (), hw, p["w_kv"].shape[1] - hw, F32, "kv_v",
                                 bf16_copy=True)
        return (k_p, v_p), (k_s, v_s), (k_bf, v_bf)

    pend = None
    for l in range(depth):
        g = p["norm_g"][l]
        if pend is None:
            xn, xns = both(_rms_cast, xp, xs, g[0])
            h, hs = _gate_up(xn, xns, p["ffn_w_gate"], p["ffn_w_up"], (l, 0))
        else:
            xp, xs, h, hs = ffn_up(xp, xs, pend, (l, 0))
        if l == n_a:
            kv_p, kv_s, kv_bf = shared_kv(xp, xs)
        y, ys = _matmul(h, hs, p["ffn_w_down"], (l, 0), 0, d, BF16, "ffn_down")
        xp, xs, xn, xns = resid(xp, xs, y, ys, g[1], g[2], 0.5)
        if l < n_a:
            proj, projs = _matmul(xn, xns, w_in_t, (l,), 0, 4 * d, BF16, "mlstm_in", w_transposed=True)
            w_gates = jnp.pad(w_in_t[l, 4 * d:, :], ((0, V7X_LANES - 2 * heads_a), (0, 0)))
            gates, gatess = _matmul(xn, xns, w_gates, (), 0, V7X_LANES, F32, "mlstm_gates", w_transposed=True)
            bias = jnp.pad(p["a_b_if"][l], (0, V7X_LANES - 2 * heads_a)).reshape(1, V7X_LANES)
            hg = p["a_head_g"][l]
            hm, c_p, n_p, m_p = _mlstm(proj, gates, bias, hg, None, l, bp, tp, tp, min(MLSTM_CHUNK, tp))
            hms, c_s, n_s, m_s = _mlstm(pad_seqs(projs), pad_seqs(gatess), bias, hg, state, l, bd, ts, td, ts)
            hms = hms.reshape(bd, ts, d)[:, :td].reshape(bd * td, d)
            for acc, val in zip(states, (c_p, n_p, m_p, c_s, n_s, m_s)):
                acc.append(val)
            y, ys = _matmul(hm, hms, p["a_w_out"], (l,), 0, d, BF16, "mlstm_out")
        else:
            j = l - n_a
            q, qs = _matmul(xn, xns, p["b_w_q"], (j,), 0, hw, BF16, "dattn_q")
            lam0 = _lambda_init(l)
            o = _dattn_prompt(q, kv_bf[0], kv_bf[1], p["b_lambda"][j], p["b_head_g"][j], bp, tp, lam0)
            os_ = _dattn_sample(qs, cache_k, cache_v, page_table, kv_s[0], kv_s[1],
                                p["b_lambda"][j], p["b_head_g"][j], td, lam0)
            y, ys = _matmul(o, os_, p["b_w_o"], (j,), 0, d, BF16, "dattn_o")
        xp, xs, h, hs = ffn_up(xp, xs, (y, ys, g[3], g[4], 1.0), (l, 1))
        y, ys = _matmul(h, hs, p["ffn_w_down"], (l, 1), 0, d, BF16, "ffn_down")
        if l + 1 < depth:
            pend = (y, ys, g[5], p["norm_g"][l + 1, 0], 0.5)
        else:
            xp, xs, _, _ = resid(xp, xs, y, ys, g[5], None, 0.5)
            if n_a == depth:
                kv_p, kv_s, kv_bf = shared_kv(xp, xs)
    return xp, xs, [jnp.stack(s) for s in states], kv_p, kv_s


def kernel(x_prompt, x_sample, state_C, state_n, state_m, cache_k, cache_v, page_table, norm_g,
           ffn_w_gate, ffn_w_up, ffn_w_down, a_w_in, a_b_if, a_head_g, a_w_out, kv_norm_g, w_kv,
           b_w_q, b_lambda, b_head_g, b_w_o):
    p = dict(norm_g=norm_g, ffn_w_gate=ffn_w_gate, ffn_w_up=ffn_w_up, ffn_w_down=ffn_w_down,
             a_w_in=a_w_in, a_b_if=a_b_if, a_head_g=a_head_g, a_w_out=a_w_out, kv_norm_g=kv_norm_g,
             w_kv=w_kv, b_w_q=b_w_q, b_lambda=b_lambda, b_head_g=b_head_g, b_w_o=b_w_o)
    bp, tp, d = x_prompt.shape
    bd, td, _ = x_sample.shape
    heads_b, dv = b_head_g.shape[1:]
    dk = dv // 2
    assert td <= SAMPLE_ROWS
    y_p, y_s, (c_p, n_p, m_p, c_s, n_s, m_s), (k_p, v_p), (k_s, v_s) = _trunk(
        x_prompt.reshape(bp * tp, d), x_sample.reshape(bd * td, d), bp, tp, bd, td,
        (state_C, state_n, state_m), cache_k, cache_v, page_table, p)
    return (y_p.reshape(bp, tp, d), y_s.reshape(bd, td, d), c_p, n_p, m_p,
            k_p.reshape(bp, tp, heads_b, 2, dk), v_p.reshape(bp, tp, heads_b, dv),
            c_s, n_s, m_s,
            k_s[0].reshape(bd, td, heads_b, 2, dk), v_s[0].reshape(bd, td, heads_b, dv))
```

```python
import functools
import math

import numpy as np
import jax
import jax.numpy as jnp
from jax import lax
from jax.experimental import pallas as pl
from jax.experimental.pallas import tpu as pltpu

EPS = 1e-6
LOG2E = math.log2(math.e)
BF16 = jnp.bfloat16
F32 = jnp.float32

V7X_LANES = 128
V7X_MXU_COLS = 256
V7X_VMEM_BYTES = 64 * 1024 * 1024
VMEM_LIMIT_BYTES = V7X_VMEM_BYTES - 6 * 1024 * 1024

SAMPLE_ROWS = 16
ROW_TILE = 1024
COL_TILE = V7X_MXU_COLS
NORM_ROWS = 256
MLSTM_CHUNK = 256
ATTN_Q_TILE = 256
SOFTMAX_STRIP = 16
SAMPLE_PAGES_PER_STEP = 4

_NT = (((1,), (1,)), ((), ()))
_TN = (((0,), (0,)), ((), ()))
_NN = (((1,), (0,)), ((), ()))


def _params(semantics):
    return pltpu.CompilerParams(dimension_semantics=semantics, vmem_limit_bytes=VMEM_LIMIT_BYTES)


def _rows_spec(arr, block, index_map, **kwargs):
    if arr.ndim == 3:
        return pl.BlockSpec((None,) + block, lambda *g: (0,) + index_map(*g), **kwargs)
    return pl.BlockSpec(block, index_map, **kwargs)


def _rms(x, g):
    return x * lax.rsqrt(jnp.mean(x * x, axis=-1, keepdims=True) + EPS) * g


def _sigmoid(x):
    return 1.0 / (1.0 + jnp.exp(-x))


def _rms_cast_kernel(x_ref, g_ref, o_ref):
    o_ref[...] = _rms(x_ref[...], g_ref[...]).astype(o_ref.dtype)


def _rms_cast(x, g):
    m, d = x.shape
    tr = min(NORM_ROWS, m)
    return pl.pallas_call(
        _rms_cast_kernel,
        out_shape=jax.ShapeDtypeStruct((m, d), BF16),
        grid=(m // tr,),
        in_specs=[pl.BlockSpec((tr, d), lambda i: (i, 0)),
                  pl.BlockSpec((1, d), lambda i: (0, 0))],
        out_specs=pl.BlockSpec((tr, d), lambda i: (i, 0)),
        compiler_params=_params(("parallel",)),
        name="rms_cast",
    )(x, g.reshape(1, d))


def _resid_norm_kernel(x_ref, y_ref, gp_ref, gn_ref, xo_ref, xn_ref, *, coef):
    x = x_ref[...] + coef * _rms(y_ref[...].astype(F32), gp_ref[...])
    xo_ref[...] = x
    xn_ref[...] = _rms(x, gn_ref[...]).astype(xn_ref.dtype)


def _resid_kernel(x_ref, y_ref, gp_ref, xo_ref, *, coef):
    xo_ref[...] = x_ref[...] + coef * _rms(y_ref[...].astype(F32), gp_ref[...])


def _resid_norm(x, y, g_post, g_next, coef):
    m, d = x.shape
    tr = min(NORM_ROWS, m)
    row = pl.BlockSpec((tr, d), lambda i: (i, 0))
    y_row = _rows_spec(y, (tr, d), lambda i: (i, 0))
    vec = pl.BlockSpec((1, d), lambda i: (0, 0))
    if g_next is None:
        return pl.pallas_call(
            functools.partial(_resid_kernel, coef=coef),
            out_shape=jax.ShapeDtypeStruct((m, d), F32),
            grid=(m // tr,),
            in_specs=[row, y_row, vec],
            out_specs=row,
            compiler_params=_params(("parallel",)),
            name="resid",
        )(x, y, g_post.reshape(1, d)), None
    return pl.pallas_call(
        functools.partial(_resid_norm_kernel, coef=coef),
        out_shape=(jax.ShapeDtypeStruct((m, d), F32), jax.ShapeDtypeStruct((m, d), BF16)),
        grid=(m // tr,),
        in_specs=[row, y_row, vec, vec],
        out_specs=(row, row),
        compiler_params=_params(("parallel",)),
        name="resid_norm",
    )(x, y, g_post.reshape(1, d), g_next.reshape(1, d))


def _sample_rows(first_row_tile, compute, os_ref):
    @pl.when(first_row_tile)
    def _():
        os_ref[...] = compute().astype(os_ref.dtype)

    @pl.when(jnp.logical_not(first_row_tile))
    def _():
        os_ref[...] = jnp.zeros_like(os_ref)


def _mm_kernel(x_ref, xs_ref, w_ref, *out_refs, w_dims):
    o_ref, os_ref = out_refs[:2]
    w = w_ref[...].astype(BF16)
    y = lax.dot_general(x_ref[...], w, w_dims, preferred_element_type=F32)
    o_ref[...] = y.astype(o_ref.dtype)
    if len(out_refs) == 3:
        out_refs[2][...] = y.astype(BF16)
    _sample_rows(pl.program_id(0) == 0,
                 lambda: lax.dot_general(xs_ref[...], w, w_dims, preferred_element_type=F32), os_ref)


def _col_tile(k, n):
    tn = min(COL_TILE, n)
    while n % (2 * tn) == 0 and k * 2 * tn * 4 <= (8 << 20):
        tn *= 2
    return tn


def _matmul(x, xs, w, lead, n0, n, out_dtype, name, w_transposed=False, bf16_copy=False):
    m, k = x.shape
    ms = xs.shape[-2]
    tm = min(ROW_TILE, m)
    tn = _col_tile(k, n)
    assert m % tm == 0 and n % tn == 0 and n0 % tn == 0
    jb0 = n0 // tn
    nlead = len(lead)
    if w_transposed:
        w_spec = pl.BlockSpec((None,) * nlead + (tn, k), lambda i, j: lead + (j + jb0, 0))
    else:
        w_spec = pl.BlockSpec((None,) * nlead + (k, tn), lambda i, j: lead + (0, j + jb0))
    x_mode = dict(pipeline_mode=pl.Buffered(1)) if tm * k * 2 > (8 << 20) else {}
    tile = pl.BlockSpec((tm, tn), lambda i, j: (i, j))
    out_shape = [jax.ShapeDtypeStruct((m, n), out_dtype), jax.ShapeDtypeStruct((m // tm, ms, n), out_dtype)]
    out_specs = [tile, pl.BlockSpec((None, ms, tn), lambda i, j: (i, 0, j))]
    if bf16_copy:
        out_shape.append(jax.ShapeDtypeStruct((m, n), BF16))
        out_specs.append(tile)
    outs = pl.pallas_call(
        functools.partial(_mm_kernel, w_dims=_NT if w_transposed else _NN),
        out_shape=tuple(out_shape),
        grid=(m // tm, n // tn),
        in_specs=[pl.BlockSpec((tm, k), lambda i, j: (i, 0), **x_mode),
                  _rows_spec(xs, (ms, k), lambda i, j: (0, 0), **x_mode),
                  w_spec],
        out_specs=tuple(out_specs),
        compiler_params=_params(("arbitrary", "arbitrary")),
        name=name,
    )(x, xs, w)
    return outs


def _gate_up_kernel(x_ref, xs_ref, wg_ref, wu_ref, o_ref, os_ref):
    wg = wg_ref[...].astype(BF16)
    wu = wu_ref[...].astype(BF16)

    def swiglu(x):
        g = jnp.dot(x, wg, preferred_element_type=F32)
        u = jnp.dot(x, wu, preferred_element_type=F32)
        return g * _sigmoid(g) * u

    o_ref[...] = swiglu(x_ref[...]).astype(o_ref.dtype)
    _sample_rows(pl.program_id(0) == 0, lambda: swiglu(xs_ref[...]), os_ref)


def _gate_up(x, xs, w_gate, w_up, lead):
    m, k = x.shape
    ms = xs.shape[-2]
    f = w_gate.shape[-1]
    tm = min(ROW_TILE, m)
    tn = COL_TILE
    assert m % tm == 0 and f % tn == 0
    w_spec = pl.BlockSpec((None, None, k, tn), lambda i, j: lead + (0, j))
    return pl.pallas_call(
        _gate_up_kernel,
        out_shape=(jax.ShapeDtypeStruct((m, f), BF16), jax.ShapeDtypeStruct((m // tm, ms, f), BF16)),
        grid=(m // tm, f // tn),
        in_specs=[pl.BlockSpec((tm, k), lambda i, j: (i, 0)),
                  _rows_spec(xs, (ms, k), lambda i, j: (0, 0)), w_spec, w_spec],
        out_specs=(pl.BlockSpec((tm, tn), lambda i, j: (i, j)),
                   pl.BlockSpec((None, ms, tn), lambda i, j: (i, 0, j))),
        compiler_params=_params(("arbitrary", "arbitrary")),
        name="ffn_gate_up",
    )(x, xs, w_gate, w_up)


def _mlstm_kernel(*refs, chunk, valid, heads, has_state, scale):
    if has_state:
        (q_ref, k_ref, v_ref, o_ref, gc_ref, bias_ref, hg_ref, c0_ref, n0_ref, m0_ref,
         h_ref, c_ref, n_ref, m_ref) = refs
    else:
        (q_ref, k_ref, v_ref, o_ref, gc_ref, bias_ref, hg_ref,
         h_ref, c_ref, n_ref, m_ref) = refs
    hh = pl.program_id(1)
    cc = pl.program_id(2)
    L = chunk

    @pl.when(cc == 0)
    def _init():
        if has_state:
            c_ref[...] = c0_ref[...]
            n_ref[...] = n0_ref[...]
            m_ref[...] = m0_ref[...]
        else:
            c_ref[...] = jnp.zeros_like(c_ref)
            n_ref[...] = jnp.zeros_like(n_ref)
            m_ref[...] = jnp.zeros_like(m_ref)

    gc = gc_ref[...] + bias_ref[...]
    lane = lax.broadcasted_iota(jnp.int32, gc.shape, 1)
    li_col = jnp.sum(jnp.where(lane == hh, gc, 0.0), axis=1, keepdims=True)
    f_col = jnp.sum(jnp.where(lane == hh + heads, gc, 0.0), axis=1, keepdims=True)
    lf_col = jnp.minimum(f_col, 0.0) - jnp.log(1.0 + jnp.exp(-jnp.abs(f_col)))
    if valid < L:
        rid = lax.broadcasted_iota(jnp.int32, (L, 1), 0)
        li_col = jnp.where(rid < valid, li_col, -jnp.inf)
        lf_col = jnp.where(rid < valid, lf_col, 0.0)

    r = lax.broadcasted_iota(jnp.int32, (L, L), 0)
    s = lax.broadcasted_iota(jnp.int32, (L, L), 1)
    causal = s <= r
    li_row = jnp.sum(jnp.where(r == s, li_col, 0.0), axis=0, keepdims=True)
    lf_row = jnp.sum(jnp.where(r == s, lf_col, 0.0), axis=0, keepdims=True)
    b_col = jnp.sum(jnp.where(causal, lf_row, 0.0), axis=1, keepdims=True)
    b_row = jnp.sum(jnp.where(r <= s, lf_col, 0.0), axis=0, keepdims=True)

    m_prev = m_ref[...]
    m_inter = b_col + m_prev
    log_d = jnp.where(causal, b_col - b_row + li_row, -jnp.inf)
    m_t = jnp.maximum(m_inter, jnp.max(log_d, axis=1, keepdims=True))
    decay_mat = jnp.exp(log_d - m_t)
    inter = jnp.exp(m_inter - m_t)

    q = q_ref[...]
    k = k_ref[...]
    v = v_ref[...]
    c_state = c_ref[...]
    n_state = n_ref[...]
    smat = lax.dot_general(q, k, _NT, preferred_element_type=F32) * scale * decay_mat
    q_c = lax.dot_general(q, c_state.astype(BF16), _NT, preferred_element_type=F32)
    num = jnp.dot(smat.astype(BF16), v, preferred_element_type=F32) + inter * q_c
    q_n = jnp.sum(q.astype(F32) * n_state, axis=1, keepdims=True)
    den = jnp.sum(smat, axis=1, keepdims=True) + inter * q_n
    hval = num / jnp.maximum(jnp.abs(den), jnp.exp(-m_t))

    hg = hg_ref[pl.ds(hh, 1), :]
    h_ref[...] = (_rms(hval, hg) * _sigmoid(o_ref[...].astype(F32))).astype(h_ref.dtype)

    b_last = jnp.sum(lf_col, axis=0, keepdims=True)
    log_w = b_last - b_col + li_col
    m_new = jnp.maximum(b_last + m_prev, jnp.max(log_w, axis=0, keepdims=True))
    w_col = jnp.exp(log_w - m_new)
    decay = jnp.exp(b_last + m_prev - m_new)
    k_scaled = k.astype(F32) * scale
    v_w = (v.astype(F32) * w_col).astype(BF16)
    c_ref[...] = decay * c_state + lax.dot_general(v_w, k_scaled.astype(BF16), _TN,
                                                   preferred_element_type=F32)
    n_ref[...] = decay * n_state + jnp.sum(k_scaled * w_col, axis=0, keepdims=True)
    m_ref[...] = m_new


def _mlstm(proj, gates, bias, head_g, state, layer, bsz, seq, valid, chunk):
    heads, dh = head_g.shape
    d = heads * dh
    nc = seq // chunk
    assert seq % chunk == 0
    has_state = state is not None

    def col(part):
        return _rows_spec(proj, (chunk, dh), lambda b, h, c: (b * nc + c, part * heads + h))

    c_spec = pl.BlockSpec((None, None, dh, dh), lambda b, h, c: (b, h, 0, 0))
    n_spec = pl.BlockSpec((None, None, 1, dh), lambda b, h, c: (b, h, 0, 0))
    m_spec = pl.BlockSpec((None, None, 1, 1), lambda b, h, c: (b, h, 0, 0))
    in_specs = [col(0), col(1), col(2), col(3),
                _rows_spec(gates, (chunk, V7X_LANES), lambda b, h, c: (b * nc + c, 0)),
                pl.BlockSpec((1, V7X_LANES), lambda b, h, c: (0, 0)),
                pl.BlockSpec((heads, dh), lambda b, h, c: (0, 0))]
    args = [proj, proj, proj, proj, gates, bias, head_g]
    if has_state:
        c0, n0, m0 = state
        n_layers = c0.shape[0]
        in_specs += [pl.BlockSpec((None, None, None, dh, dh), lambda b, h, c: (layer, b, h, 0, 0)),
                     pl.BlockSpec((None, None, None, 1, dh), lambda b, h, c: (layer, b, h, 0, 0)),
                     pl.BlockSpec((None, None, None, 1, 1), lambda b, h, c: (layer, b, h, 0, 0))]
        args += [c0, n0.reshape(n_layers, bsz, heads, 1, dh), m0.reshape(n_layers, bsz, heads, 1, 1)]
    h, c_new, n_new, m_new = pl.pallas_call(
        functools.partial(_mlstm_kernel, chunk=chunk, valid=valid, heads=heads,
                          has_state=has_state, scale=dh ** -0.5),
        out_shape=(jax.ShapeDtypeStruct((bsz * seq, d), BF16),
                   jax.ShapeDtypeStruct((bsz, heads, dh, dh), F32),
                   jax.ShapeDtypeStruct((bsz, heads, 1, dh), F32),
                   jax.ShapeDtypeStruct((bsz, heads, 1, 1), F32)),
        grid=(bsz, heads, nc),
        in_specs=in_specs,
        out_specs=(pl.BlockSpec((chunk, dh), lambda b, h, c: (b * nc + c, h)),
                   c_spec, n_spec, m_spec),
        compiler_params=_params(("parallel", "parallel", "arbitrary")),
        name="mlstm",
    )(*args)
    return h, c_new, n_new.reshape(bsz, heads, dh), m_new.reshape(bsz, heads)


def _lambda(lam_ref, lam0):
    lp = lam_ref[...]
    return (jnp.exp(jnp.sum(lp[0:1] * lp[1:2], axis=1, keepdims=True))
            - jnp.exp(jnp.sum(lp[2:3] * lp[3:4], axis=1, keepdims=True)) + lam0)


def _alibi_slopes(heads):
    return np.asarray(2.0 ** (-8.0 * np.arange(1, heads + 1) / heads), np.float32)


def _dattn_prompt_kernel(slopes_ref, q_ref, k_ref, v_ref, lam_ref, hg_ref, o_ref,
                         s_scr, p_scr, cb_scr, l_scr, *, tq, nq, dk, lam0):
    hh = pl.program_id(1)
    qi = pl.program_id(2)
    lam = _lambda(lam_ref, lam0)
    slope = slopes_ref[hh]
    hg = hg_ref[pl.ds(hh, 1), :]
    n_strips = tq // SOFTMAX_STRIP

    def attend(n_keys):
        q = q_ref[...]
        n_past = n_keys - tq
        key_pos = lax.broadcasted_iota(jnp.int32, (SOFTMAX_STRIP, n_keys), 1).astype(F32)
        cb_scr[:, 0:n_keys] = (slope * LOG2E) * key_pos
        for j in range(2):
            s_scr[j, :, 0:n_keys] = lax.dot_general(q[:, j * dk:(j + 1) * dk],
                                                    k_ref[0:n_keys, j * dk:(j + 1) * dk], _NT,
                                                    preferred_element_type=F32)

        def strip(r, carry):
            r0 = pl.multiple_of(r * SOFTMAX_STRIP, SOFTMAX_STRIP)
            keep = (lax.broadcasted_iota(jnp.int32, (SOFTMAX_STRIP, tq), 1)
                    <= r0 + lax.broadcasted_iota(jnp.int32, (SOFTMAX_STRIP, tq), 0))
            for j in range(2):
                t = s_scr[j, pl.ds(r0, SOFTMAX_STRIP), 0:n_keys] * (dk ** -0.5 * LOG2E) + cb_scr[:, 0:n_keys]
                own = jnp.where(keep, t[:, n_past:], -jnp.inf)
                t = jnp.concatenate([t[:, :n_past], own], axis=1) if n_past else own
                e = jnp.exp2(t - jnp.max(t, axis=1, keepdims=True))
                l_scr[j, pl.ds(r0, SOFTMAX_STRIP), :] = jnp.sum(e, axis=1, keepdims=True)
                p_scr[j, pl.ds(r0, SOFTMAX_STRIP), 0:n_keys] = e.astype(BF16)
            return carry

        lax.fori_loop(0, n_strips, strip, 0, unroll=True)
        outs = [jnp.dot(p_scr[j, :, 0:n_keys], v_ref[0:n_keys, :], preferred_element_type=F32) / l_scr[j]
                for j in range(2)]
        o = outs[0] - lam * outs[1]
        o_ref[...] = (_rms(o, hg) * (1.0 - lam0)).astype(o_ref.dtype)

    for i in range(nq):
        pl.when(qi == i)(functools.partial(attend, (i + 1) * tq))


def _dattn_prompt(q, k, v, lam_p, head_g, bsz, seq, lam0):
    heads, dv = head_g.shape
    dk = dv // 2
    tq = min(ATTN_Q_TILE, seq)
    nq = seq // tq
    kv_spec = pl.BlockSpec((seq, dv), lambda b, h, i: (b, h))
    q_spec = pl.BlockSpec((tq, dv), lambda b, h, i: (b * nq + i, h))
    return pl.pallas_call(
        functools.partial(_dattn_prompt_kernel, tq=tq, nq=nq, dk=dk, lam0=lam0),
        out_shape=jax.ShapeDtypeStruct(q.shape, BF16),
        grid=(bsz, heads, nq),
        in_specs=[pl.BlockSpec(memory_space=pltpu.SMEM),
                  q_spec, kv_spec, kv_spec,
                  pl.BlockSpec((4, dk), lambda b, h, i: (0, 0)),
                  pl.BlockSpec((heads, dv), lambda b, h, i: (0, 0))],
        out_specs=q_spec,
        scratch_shapes=[pltpu.VMEM((2, tq, seq), F32), pltpu.VMEM((2, tq, seq), BF16),
                        pltpu.VMEM((SOFTMAX_STRIP, seq), F32), pltpu.VMEM((2, tq, 1), F32)],
        compiler_params=_params(("parallel", "parallel", "arbitrary")),
        name="dattn_prompt",
    )(jnp.asarray(_alibi_slopes(heads)), q, k, v, lam_p, head_g)


def _dattn_sample_kernel(pt_ref, qt_ref, *refs, dk, page, n_pages, pages_per_step, td, lam0):
    kc_refs, vc_refs = refs[:pages_per_step], refs[pages_per_step:2 * pages_per_step]
    (kn_ref, vn_ref, apast_ref, anew_ref, slope_ref, lam_ref, gain_ref,
     o_ref, m_scr, l_scr, acc_scr) = refs[2 * pages_per_step:]
    pg = pl.program_id(1)
    n_steps = n_pages // pages_per_step
    n_cols = qt_ref.shape[1]
    n_rows = vn_ref.shape[0]
    past_len = n_pages * page

    def to_col(row):
        eye = (lax.broadcasted_iota(jnp.int32, (n_cols, n_cols), 0)
               == lax.broadcasted_iota(jnp.int32, (n_cols, n_cols), 1))
        return jnp.sum(jnp.where(eye, row, 0.0), axis=1, keepdims=True)

    @pl.when(pg == 0)
    def _init():
        m_scr[...] = jnp.full_like(m_scr, -jnp.inf)
        l_scr[...] = jnp.zeros_like(l_scr)
        acc_scr[...] = jnp.zeros_like(acc_scr)

    def process(k_ref, v_ref, a_ref, page_bias):
        kb = jnp.concatenate([k_ref[pl.ds(0, n_rows, stride=2), :].astype(BF16),
                              k_ref[pl.ds(1, n_rows, stride=2), :].astype(BF16)], axis=1)
        sc = jnp.dot(kb, qt_ref[...], preferred_element_type=F32) * (dk ** -0.5) + a_ref[...] + page_bias
        m_old = m_scr[...]
        m_new = jnp.maximum(m_old, jnp.max(sc, axis=0, keepdims=True))
        alpha = jnp.exp(m_old - m_new)
        p = jnp.exp(sc - m_new)
        l_scr[...] = alpha * l_scr[...] + jnp.sum(p, axis=0, keepdims=True)
        m_scr[...] = m_new
        pv = lax.dot_general(p.astype(BF16), v_ref[...].astype(BF16), _TN, preferred_element_type=F32)
        acc_scr[...] = to_col(alpha) * acc_scr[...] + pv

    @pl.when(pg < n_steps)
    def _past():
        for u in range(pages_per_step):
            key0 = (pg * pages_per_step + u) * page
            process(kc_refs[u], vc_refs[u], apast_ref, slope_ref[...] * (key0 - past_len).astype(F32))

    @pl.when(pg == n_steps)
    def _new():
        process(kn_ref, vn_ref, anew_ref, 0.0)
        lam = _lambda(lam_ref, lam0)
        on = acc_scr[...] / to_col(l_scr[...])
        o = on - lam * pltpu.roll(on, n_cols - td, 0)
        o_ref[...] = _rms(o, gain_ref[...]) * (1.0 - lam0)


def _dattn_sample(q, cache_k, cache_v, page_table, k_new, v_new, lam_p, head_g, td, lam0):
    heads, dv = head_g.shape
    dk = dv // 2
    bd, n_pages = page_table.shape
    n_pool, page = cache_k.shape[:2]
    hw = heads * dv
    q, k_new, v_new = (a[0] if a.ndim == 3 else a for a in (q, k_new, v_new))
    n_real = heads * 2 * td
    n_cols = -(-n_real // V7X_LANES) * V7X_LANES
    n_rows = page * heads
    pps = math.gcd(n_pages, SAMPLE_PAGES_PER_STEP)
    n_steps = n_pages // pps

    col = np.arange(n_cols)
    real = col < n_real
    col_h = np.where(real, col // (2 * td), -1)
    col_j = (col // td) % 2
    col_t = col % td
    qt = q.reshape(bd, td, heads * 2, dk).transpose(0, 3, 2, 1).reshape(bd, dk, n_real)
    qt = jnp.pad(qt, ((0, 0), (0, 0), (0, n_cols - n_real)))
    qt = jnp.concatenate([jnp.where(col_j == j, qt, jnp.zeros_like(qt)) for j in range(2)], axis=1)

    def page_rows(new, width):
        new = jnp.pad(new.reshape(bd, td, hw), ((0, 0), (0, page - td), (0, 0)))
        return new.reshape(-1, width)

    col_slope = np.where(real, _alibi_slopes(heads)[np.maximum(col_h, 0)], 0.0).astype(np.float32)
    row = np.arange(n_rows)
    row_tok, row_h = row // heads, row % heads
    rel = (col_t[None, :] - row_tok[:, None]).astype(np.float32)
    match = (row_h[:, None] == col_h[None, :]) | ~real[None, :]
    a_past = np.where(match, -col_slope[None, :] * rel, -np.inf).astype(np.float32)
    a_new = np.where(match & ((rel >= 0) | ~real[None, :]), -col_slope[None, :] * rel, -np.inf).astype(np.float32)
    gain = jnp.pad(jnp.repeat(head_g, 2 * td, axis=0), ((0, n_cols - n_real), (0, 0)), constant_values=1.0)

    def page_map(u):
        return lambda b, p, pt: (pt[b * n_pages + jnp.minimum(p, n_steps - 1) * pps + u], 0)

    const = lambda b, p, pt: (0, 0)
    once = dict(pipeline_mode=pl.Buffered(1))
    k2d = cache_k.reshape(n_pool * 2 * n_rows, dk)
    v2d = cache_v.reshape(n_pool * n_rows, dv)
    out = pl.pallas_call(
        functools.partial(_dattn_sample_kernel, dk=dk, page=page, n_pages=n_pages, pages_per_step=pps,
                          td=td, lam0=lam0),
        out_shape=jax.ShapeDtypeStruct((bd, n_cols, dv), F32),
        grid_spec=pltpu.PrefetchScalarGridSpec(
            num_scalar_prefetch=1,
            grid=(bd, n_steps + 1),
            in_specs=([pl.BlockSpec((None, 2 * dk, n_cols), lambda b, p, pt: (b, 0, 0))]
                      + [pl.BlockSpec((2 * n_rows, dk), page_map(u)) for u in range(pps)]
                      + [pl.BlockSpec((n_rows, dv), page_map(u)) for u in range(pps)]
                      + [pl.BlockSpec((2 * n_rows, dk), lambda b, p, pt: (b, 0), **once),
                         pl.BlockSpec((n_rows, dv), lambda b, p, pt: (b, 0), **once),
                         pl.BlockSpec((n_rows, n_cols), const, **once),
                         pl.BlockSpec((n_rows, n_cols), const, **once),
                         pl.BlockSpec((1, n_cols), const),
                         pl.BlockSpec((4, dk), const),
                         pl.BlockSpec((n_cols, dv), const)]),
            out_specs=pl.BlockSpec((None, n_cols, dv), lambda b, p, pt: (b, 0, 0)),
            scratch_shapes=[pltpu.VMEM((1, n_cols), F32), pltpu.VMEM((1, n_cols), F32),
                            pltpu.VMEM((n_cols, dv), F32)]),
        compiler_params=_params(("parallel", "arbitrary")),
        name="dattn_sample",
    )(page_table.reshape(-1), qt, *([k2d] * pps), *([v2d] * pps),
      page_rows(k_new, dk), page_rows(v_new, dv),
      jnp.asarray(a_past), jnp.asarray(a_new), jnp.asarray(col_slope.reshape(1, n_cols)), lam_p, gain)
    o = out[:, :n_real].reshape(bd, heads, 2, td, dv)[:, :, 0].transpose(0, 2, 1, 3)
    return o.reshape(bd * td, hw).astype(BF16)


def _lambda_init(layer):
    return 0.8 - 0.6 * math.exp(-0.3 * layer)


def _trunk(xp, xs, bp, tp, bd, td, state, cache_k, cache_v, page_table, p):
    depth = p["norm_g"].shape[0]
    n_a = p["a_w_in"].shape[0]
    d = xp.shape[1]
    heads_a = p["a_head_g"].shape[1]
    hw = p["b_w_q"].shape[2]
    ts = SAMPLE_ROWS
    states = ([], [], [], [], [], [])
    kv_p = kv_s = kv_bf = None
    w_in_t = p["a_w_in"].transpose(0, 2, 1)

    def both(fn, a, b, *args):
        return fn(a, *args), fn(b, *args)

    def pad_seqs(a):
        a = a[0] if a.ndim == 3 else a
        return jnp.pad(a.reshape(bd, td, -1), ((0, 0), (0, ts - td), (0, 0))).reshape(bd * ts, -1)

    def resid(x, xs_, y, ys_, g_post, g_next, coef):
        (x, xn), (xs_, xns) = _resid_norm(x, y, g_post, g_next, coef), _resid_norm(xs_, ys_, g_post, g_next, coef)
        return x, xs_, xn, xns

    def ffn_up(x, xs_, pend, lead):
        x, xs_, xn, xns = resid(x, xs_, *pend)
        h, hs = _gate_up(xn, xns, p["ffn_w_gate"], p["ffn_w_up"], lead)
        return x, xs_, h, hs

    def shared_kv(x, xs_):
        xkv, xkvs = both(_rms_cast, x, xs_, p["kv_norm_g"])
        k_p, k_s, k_bf = _matmul(xkv, xkvs, p["w_kv"], (), 0, hw, F32, "kv_k", bf16_copy=True)
        v_p, v_s, v_bf = _matmul(xkv, xkvs, p["w_kv"], (), hw, p["w_kv"].shape[1] - hw, F32, "kv_v",
                                 bf16_copy=True)
        return (k_p, v_p), (k_s, v_s), (k_bf, v_bf)

    pend = None
    for l in range(depth):
        g = p["norm_g"][l]
        if pend is None:
            xn, xns = both(_rms_cast, xp, xs, g[0])
            h, hs = _gate_up(xn, xns, p["ffn_w_gate"], p["ffn_w_up"], (l, 0))
        else:
            xp, xs, h, hs = ffn_up(xp, xs, pend, (l, 0))
        if l == n_a:
            kv_p, kv_s, kv_bf = shared_kv(xp, xs)
        y, ys = _matmul(h, hs, p["ffn_w_down"], (l, 0), 0, d, BF16, "ffn_down")
        xp, xs, xn, xns = resid(xp, xs, y, ys, g[1], g[2], 0.5)
        if l < n_a:
            proj, projs = _matmul(xn, xns, w_in_t, (l,), 0, 4 * d, BF16, "mlstm_in", w_transposed=True)
            w_gates = jnp.pad(w_in_t[l, 4 * d:, :], ((0, V7X_LANES - 2 * heads_a), (0, 0)))
            gates, gatess = _matmul(xn, xns, w_gates, (), 0, V7X_LANES, F32, "mlstm_gates", w_transposed=True)
            bias = jnp.pad(p["a_b_if"][l], (0, V7X_LANES - 2 * heads_a)).reshape(1, V7X_LANES)
            hg = p["a_head_g"][l]
            hm, c_p, n_p, m_p = _mlstm(proj, gates, bias, hg, None, l, bp, tp, tp, min(MLSTM_CHUNK, tp))
            hms, c_s, n_s, m_s = _mlstm(pad_seqs(projs), pad_seqs(gatess), bias, hg, state, l, bd, ts, td, ts)
            hms = hms.reshape(bd, ts, d)[:, :td].reshape(bd * td, d)
            for acc, val in zip(states, (c_p, n_p, m_p, c_s, n_s, m_s)):
                acc.append(val)
            y, ys = _matmul(hm, hms, p["a_w_out"], (l,), 0, d, BF16, "mlstm_out")
        else:
            j = l - n_a
            q, qs = _matmul(xn, xns, p["b_w_q"], (j,), 0, hw, BF16, "dattn_q")
            lam0 = _lambda_init(l)
            o = _dattn_prompt(q, kv_bf[0], kv_bf[1], p["b_lambda"][j], p["b_head_g"][j], bp, tp, lam0)
            os_ = _dattn_sample(qs, cache_k, cache_v, page_table, kv_s[0], kv_s[1],
                                p["b_lambda"][j], p["b_head_g"][j], td, lam0)
            y, ys = _matmul(o, os_, p["b_w_o"], (j,), 0, d, BF16, "dattn_o")
        xp, xs, h, hs = ffn_up(xp, xs, (y, ys, g[3], g[4], 1.0), (l, 1))
        y, ys = _matmul(h, hs, p["ffn_w_down"], (l, 1), 0, d, BF16, "ffn_down")
        if l + 1 < depth:
            pend = (y, ys, g[5], p["norm_g"][l + 1, 0], 0.5)
        else:
            xp, xs, _, _ = resid(xp, xs, y, ys, g[5], None, 0.5)
            if n_a == depth:
                kv_p, kv_s, kv_bf = shared_kv(xp, xs)
    return xp, xs, [jnp.stack(s) for s in states], kv_p, kv_s


def kernel(x_prompt, x_sample, state_C, state_n, state_m, cache_k, cache_v, page_table, norm_g,
           ffn_w_gate, ffn_w_up, ffn_w_down, a_w_in, a_b_if, a_head_g, a_w_out, kv_norm_g, w_kv,
           b_w_q, b_lambda, b_head_g, b_w_o):
    p = dict(norm_g=norm_g, ffn_w_gate=ffn_w_gate, ffn_w_up=ffn_w_up, ffn_w_down=ffn_w_down,
             a_w_in=a_w_in, a_b_if=a_b_if, a_head_g=a_head_g, a_w_out=a_w_out, kv_norm_g=kv_norm_g,
             w_kv=w_kv, b_w_q=b_w_q, b_lambda=b_lambda, b_head_g=b_head_g, b_w_o=b_w_o)
    bp, tp, d = x_prompt.shape
    bd, td, _ = x_sample.shape
    heads_b, dv = b_head_g.shape[1:]
    dk = dv // 2
    assert td <= SAMPLE_ROWS
    y_p, y_s, (c_p, n_p, m_p, c_s, n_s, m_s), (k_p, v_p), (k_s, v_s) = _trunk(
        x_prompt.reshape(bp * tp, d), x_sample.reshape(bd * td, d), bp, tp, bd, td,
        (state_C, state_n, state_m), cache_k, cache_v, page_table, p)
    return (y_p.reshape(bp, tp, d), y_s.reshape(bd, td, d), c_p, n_p, m_p,
            k_p.reshape(bp, tp, heads_b, 2, dk), v_p.reshape(bp, tp, heads_b, dv),
            c_s, n_s, m_s,
            k_s[0].reshape(bd, td, heads_b, 2, dk), v_s[0].reshape(bd, td, heads_b, dv))
```
